```python
import math
import jax, jax.numpy as jnp
from jax import lax
import numpy as np

D_MODEL = 2048
BATCH = 2
SEQ = 16384
DEPTH = 1

MLA_HEADS = 16
MLA_Q_RANK = 512
MLA_KV_RANK = 256
MLA_NOPE_DIM = 128
MLA_ROPE_DIM = 64
MLA_V_DIM = 128
ROPE_THETA = 10000.0
SWA_Q_HEADS = 32
SWA_KV_HEADS = 4
SWA_HEAD_DIM = 64
SWA_GROUP = SWA_Q_HEADS // SWA_KV_HEADS
WINDOW = 128
BLOCK = 128
REL_BUCKETS = 32
REL_MAX_DIST = 128
D_FF = 5632
CONV_WIDTH = 3
N_BRANCHES = 2
EPS = 1e-6

MLA_WIDTH = MLA_HEADS * MLA_V_DIM
SWA_WIDTH = SWA_Q_HEADS * SWA_HEAD_DIM
SWA_KV_WIDTH = SWA_KV_HEADS * SWA_HEAD_DIM
IN_SPLITS = (MLA_Q_RANK, MLA_KV_RANK, MLA_ROPE_DIM, SWA_WIDTH, SWA_KV_WIDTH, SWA_KV_WIDTH, N_BRANCHES * D_MODEL)
IN_COLS = MLA_Q_RANK + MLA_KV_RANK + MLA_ROPE_DIM + SWA_WIDTH + 2 * SWA_KV_WIDTH + N_BRANCHES * D_MODEL

kernel_name = "hybrid_mla_swa_gated_convffn"


def rms_norm(x, g):
    xf = x.astype(jnp.float32)
    y = xf * lax.rsqrt(jnp.mean(xf * xf, axis=-1, keepdims=True) + EPS)
    return (y * g.astype(jnp.float32)).astype(x.dtype)


def rope_angles(pos, dim):
    inv = ROPE_THETA ** (-jnp.arange(0, dim, 2, dtype=jnp.float32) / dim)
    ang = pos.astype(jnp.float32)[:, None] * inv[None, :]
    return jnp.cos(ang), jnp.sin(ang)


def apply_rope(x, cos, sin):
    half = x.shape[-1] // 2
    x1, x2 = x[..., :half], x[..., half:]
    cos = cos.astype(x.dtype)
    sin = sin.astype(x.dtype)
    return jnp.concatenate([x1 * cos - x2 * sin, x2 * cos + x1 * sin], axis=-1)


def t5_bucket(dist):
    n = jnp.maximum(dist, 0)
    max_exact = REL_BUCKETS // 2
    large = max_exact + (jnp.log(jnp.maximum(n, 1).astype(jnp.float32) / max_exact)
                         / math.log(REL_MAX_DIST / max_exact)
                         * (REL_BUCKETS - max_exact)).astype(jnp.int32)
    large = jnp.minimum(large, REL_BUCKETS - 1)
    return jnp.where(n < max_exact, n, large)


def mla_attention(q_nope, q_rope, k_nope, k_rope, v):
    B, S, H, _ = q_nope.shape
    nb = S // BLOCK
    scale = (MLA_NOPE_DIM + MLA_ROPE_DIM) ** -0.5
    qn = q_nope.reshape(B, nb, BLOCK, H, MLA_NOPE_DIM).transpose(1, 0, 2, 3, 4)
    qr = q_rope.reshape(B, nb, BLOCK, H, MLA_ROPE_DIM).transpose(1, 0, 2, 3, 4)
    k_pos = jnp.arange(S)

    def one_block(args):
        qn_b, qr_b, i = args
        s = (jnp.einsum('bqhd,bkhd->bhqk', qn_b, k_nope)
             + jnp.einsum('bqhr,bkr->bhqk', qr_b, k_rope)).astype(jnp.float32) * scale
        q_pos = i * BLOCK + jnp.arange(BLOCK)
        causal = k_pos[None, :] <= q_pos[:, None]
        s = jnp.where(causal, s, -jnp.inf)
        p = jax.nn.softmax(s, axis=-1).astype(v.dtype)
        return jnp.einsum('bhqk,bkhd->bqhd', p, v)

    out = lax.map(one_block, (qn, qr, jnp.arange(nb)))
    return out.transpose(1, 0, 2, 3, 4).reshape(B, S, H * MLA_V_DIM)


def swa_attention(q, k, v, sinks, rel_table):
    B, S, _ = q.shape
    nb = S // BLOCK
    q = q.reshape(B, nb, BLOCK, SWA_KV_HEADS, SWA_GROUP, SWA_HEAD_DIM)
    k = k.reshape(B, nb, BLOCK, SWA_KV_HEADS, SWA_HEAD_DIM)
    v = v.reshape(B, nb, BLOCK, SWA_KV_HEADS, SWA_HEAD_DIM)

    def band(t):
        prev = jnp.pad(t[:, :-1], ((0, 0), (1, 0), (0, 0), (0, 0), (0, 0)))
        return jnp.concatenate([prev, t], axis=2)

    kb, vb = band(k), band(v)
    s = jnp.einsum('bnqhgd,bnkhd->bhgnqk', q, kb).astype(jnp.float32) * SWA_HEAD_DIM ** -0.5
    qi = jnp.arange(BLOCK)[:, None]
    kj = jnp.arange(2 * BLOCK)[None, :]
    dist = qi + BLOCK - kj
    in_window = (dist >= 0) & (dist < WINDOW)
    has_prev = (jnp.arange(nb)[:, None, None] > 0) | (kj >= BLOCK)[None]
    mask = in_window[None] & has_prev
    bias = rel_table.astype(jnp.float32)[t5_bucket(dist)]
    bias = bias.transpose(2, 0, 1).reshape(SWA_KV_HEADS, SWA_GROUP, 1, BLOCK, 2 * BLOCK)
    s = jnp.where(mask[None, None, None], s + bias[None], -jnp.inf)
    sink = sinks.astype(jnp.float32).reshape(SWA_KV_HEADS, SWA_GROUP)[None, :, :, None, None]
    m = jnp.maximum(jnp.max(s, axis=-1), sink)
    p = jnp.exp(s - m[..., None])
    denom = jnp.sum(p, axis=-1) + jnp.exp(sink - m)
    p = (p / denom[..., None]).astype(vb.dtype)
    out = jnp.einsum('bhgnqk,bnkhd->bnqhgd', p, vb)
    return out.reshape(B, S, SWA_WIDTH)


def hybrid_mixer(h, w_in, mla_q_norm, mla_w_q_up, mla_kv_norm, mla_w_kv_up,
                 swa_sinks, rel_table, w_o_mla, w_o_swa, w_out):
    B, S, _ = h.shape
    offsets = [int(o) for o in np.cumsum(IN_SPLITS)[:-1]]
    c_q, c_kv, k_r, q_s, k_s, v_s, gates = jnp.split(h @ w_in, offsets, axis=-1)
    pos = jnp.arange(S)
    q = (rms_norm(c_q, mla_q_norm) @ mla_w_q_up).reshape(B, S, MLA_HEADS, MLA_NOPE_DIM + MLA_ROPE_DIM)
    kv = (rms_norm(c_kv, mla_kv_norm) @ mla_w_kv_up).reshape(B, S, MLA_HEADS, MLA_NOPE_DIM + MLA_V_DIM)
    cos, sin = rope_angles(pos, MLA_ROPE_DIM)
    q_rope = apply_rope(q[..., MLA_NOPE_DIM:], cos[:, None, :], sin[:, None, :])
    k_rope = apply_rope(k_r, cos, sin)
    o_a = mla_attention(q[..., :MLA_NOPE_DIM], q_rope, kv[..., :MLA_NOPE_DIM], k_rope, kv[..., MLA_NOPE_DIM:])
    o_b = swa_attention(q_s, k_s, v_s, swa_sinks, rel_table)
    g = jax.nn.sigmoid(gates.astype(jnp.float32)).reshape(B, S, N_BRANCHES, D_MODEL)
    merged = g[:, :, 0] * (o_a @ w_o_mla).astype(jnp.float32) + g[:, :, 1] * (o_b @ w_o_swa).astype(jnp.float32)
    return merged.astype(h.dtype) @ w_out


def conv_ffn(h, w_up, conv_w, conv_b, w_down):
    S = h.shape[1]
    a, b = jnp.split(h @ w_up, 2, axis=-1)
    ap = jnp.pad(a, ((0, 0), (CONV_WIDTH - 1, 0), (0, 0)))
    c = conv_b
    for j in range(CONV_WIDTH):
        c = c + conv_w[j] * ap[:, j:j + S]
    return (jax.nn.gelu(c, approximate=True) * b) @ w_down


def setup_inputs(seed: int = 0) -> dict:
    key = jax.random.key(seed)
    ks = jax.random.split(key, 24)
    L = DEPTH
    f32 = jnp.float32

    def w(k, shape, fan_in):
        return jax.random.normal(k, shape, f32) * fan_in ** -0.5

    def gain(k, shape):
        return 1.0 + 0.05 * jax.random.normal(k, shape, f32)

    return {
        "x": jax.random.normal(ks[0], (BATCH, SEQ, D_MODEL), f32),
        "norm_mix_pre": gain(ks[1], (L, D_MODEL)),
        "norm_mix_post": gain(ks[2], (L, D_MODEL)),
        "norm_ffn_pre": gain(ks[3], (L, D_MODEL)),
        "norm_ffn_post": gain(ks[4], (L, D_MODEL)),
        "w_in": w(ks[5], (L, D_MODEL, IN_COLS), D_MODEL),
        "mla_q_norm": gain(ks[6], (L, MLA_Q_RANK)),
        "mla_w_q_up": w(ks[7], (L, MLA_Q_RANK, MLA_HEADS * (MLA_NOPE_DIM + MLA_ROPE_DIM)), MLA_Q_RANK),
        "mla_kv_norm": gain(ks[8], (L, MLA_KV_RANK)),
        "mla_w_kv_up": w(ks[9], (L, MLA_KV_RANK, MLA_HEADS * (MLA_NOPE_DIM + MLA_V_DIM)), MLA_KV_RANK),
        "swa_sinks": jax.random.normal(ks[10], (L, SWA_Q_HEADS), f32),
        "rel_bias_table": 0.5 * jax.random.normal(ks[11], (REL_BUCKETS, SWA_Q_HEADS), f32),
        "w_o_mla": w(ks[12], (L, MLA_WIDTH, D_MODEL), MLA_WIDTH),
        "w_o_swa": w(ks[13], (L, SWA_WIDTH, D_MODEL), SWA_WIDTH),
        "w_out": w(ks[14], (L, D_MODEL, D_MODEL), D_MODEL),
        "ffn_w_up": w(ks[15], (L, D_MODEL, 2 * D_FF), D_MODEL),
        "ffn_conv_w": w(ks[16], (L, CONV_WIDTH, D_FF), CONV_WIDTH),
        "ffn_conv_b": 0.01 * jax.random.normal(ks[17], (L, D_FF), f32),
        "ffn_w_down": w(ks[18], (L, D_FF, D_MODEL), D_FF),
    }


def reference(x, norm_mix_pre, norm_mix_post, norm_ffn_pre, norm_ffn_post, w_in,
              mla_q_norm, mla_w_q_up, mla_kv_norm, mla_w_kv_up, swa_sinks, rel_bias_table,
              w_o_mla, w_o_swa, w_out, ffn_w_up, ffn_conv_w, ffn_conv_b, ffn_w_down):
    for l in range(DEPTH):
        h = rms_norm(x, norm_mix_pre[l])
        y = hybrid_mixer(h, w_in[l], mla_q_norm[l], mla_w_q_up[l], mla_kv_norm[l], mla_w_kv_up[l],
                         swa_sinks[l], rel_bias_table, w_o_mla[l], w_o_swa[l], w_out[l])
        x = x + rms_norm(y, norm_mix_post[l])
        h = rms_norm(x, norm_ffn_pre[l])
        y = conv_ffn(h, ffn_w_up[l], ffn_conv_w[l], ffn_conv_b[l], ffn_w_down[l])
        x = x + rms_norm(y, norm_ffn_post[l])
    return x
```

```python
import functools
import math

import numpy as np
import jax
import jax.numpy as jnp
from jax import lax
from jax.experimental import pallas as pl
from jax.experimental.pallas import tpu as pltpu

F32 = jnp.float32
BF16 = jnp.bfloat16

D_MODEL = 2048
MLA_HEADS = 16
MLA_Q_RANK = 512
MLA_KV_RANK = 256
MLA_NOPE = 128
MLA_ROPE = 64
MLA_V = 128
MLA_QK_PAD = 256
ROPE_THETA = 10000.0
SWA_Q_HEADS = 32
SWA_KV_HEADS = 4
SWA_GROUP = 8
SWA_DH = 64
SWA_BLOCK = 128
REL_BUCKETS = 32
REL_MAX_DIST = 128
D_FF = 5632
EPS = 1e-6
NEG = -1e30
LOG2E = 1.4426950408889634

COL_GATES = 0
COL_QS = 4096
COL_CQ = 6144
COL_KS = 6656
COL_VS = 6912
COL_CKV = 7168
COL_KRA = 7424
COL_KRB = 7552
IN_COLS_PAD = 7680

VMEM_LIMIT = 56 * 1024 * 1024


def _cparams(sem):
    return pltpu.CompilerParams(dimension_semantics=sem, vmem_limit_bytes=VMEM_LIMIT)


def _rms(x, g):
    return x * lax.rsqrt(jnp.mean(x * x, axis=-1, keepdims=True) + EPS) * g


def _dot(a, b):
    return jnp.dot(a, b, preferred_element_type=F32)


def _dot_nt(a, b):
    return lax.dot_general(a, b, (((1,), (1,)), ((), ())), preferred_element_type=F32)


def _in_proj_body(x_ref, g_ref, w_ref, o_ref, hn_ref):
    @pl.when(pl.program_id(1) == 0)
    def _():
        hn_ref[...] = _rms(x_ref[...], g_ref[...]).astype(BF16)

    o_ref[...] = _dot(hn_ref[...], w_ref[...]).astype(o_ref.dtype)


def _in_proj(x2, gain, w, tm, tn):
    m = x2.shape[0]
    n = w.shape[1]
    return pl.pallas_call(
        _in_proj_body,
        grid=(m // tm, n // tn),
        in_specs=[
            pl.BlockSpec((tm, D_MODEL), lambda i, j: (i, 0)),
            pl.BlockSpec((1, D_MODEL), lambda i, j: (0, 0)),
            pl.BlockSpec((D_MODEL, tn), lambda i, j: (0, j)),
        ],
        out_specs=pl.BlockSpec((tm, tn), lambda i, j: (i, j)),
        out_shape=jax.ShapeDtypeStruct((m, n), BF16),
        scratch_shapes=[pltpu.VMEM((tm, D_MODEL), BF16)],
        compiler_params=_cparams(("parallel", "arbitrary")),
        name="in_proj",
    )(x2, gain, w)


def _qkv_up_body(cq_ref, ckv_ref, kra_ref, krb_ref, gq_ref, gkv_ref, wqt_ref, wk_ref, wvt_ref,
                 cq2_ref, sq2_ref, ck_ref, sk_ref, qt_ref, k_ref, vt_ref, *, qscale):
    ts = cq_ref.shape[0]
    cqn = _rms(cq_ref[...].astype(F32), gq_ref[...]).astype(BF16)
    ckvn = _rms(ckv_ref[...].astype(F32), gkv_ref[...]).astype(BF16)
    cq2 = cq2_ref[...]
    sq2 = sq2_ref[...]
    zeros_q = jnp.zeros((MLA_QK_PAD - MLA_NOPE - MLA_ROPE, ts), BF16)
    for h in range(MLA_HEADS):
        qf = _dot_nt(wqt_ref[h], cqn) * qscale
        rope = qf[MLA_NOPE:MLA_NOPE + MLA_ROPE] * cq2 + qf[MLA_NOPE + MLA_ROPE:] * sq2
        qt_ref[0, h, 0:MLA_NOPE, :] = qf[0:MLA_NOPE].astype(BF16)
        qt_ref[0, h, MLA_NOPE:MLA_NOPE + MLA_ROPE, :] = rope.astype(BF16)
        qt_ref[0, h, MLA_NOPE + MLA_ROPE:, :] = zeros_q
    kr = (kra_ref[...].astype(F32) * ck_ref[...] + krb_ref[...].astype(F32) * sk_ref[...]).astype(BF16)
    kall = _dot(ckvn, wk_ref[...]).astype(BF16)
    for h in range(MLA_HEADS):
        k_ref[0, h, :, 0:MLA_NOPE] = kall[:, h * MLA_NOPE:(h + 1) * MLA_NOPE]
        k_ref[0, h, :, MLA_NOPE:] = kr
        vt_ref[0, h] = _dot_nt(wvt_ref[h], ckvn).astype(BF16)


def _qkv_up(p, gq, gkv, wqt, wk, wvt, cq2, sq2, ck, sk, b, s, ts):
    ns = s // ts
    row = lambda bi, i: bi * ns + i
    qscale = float((MLA_NOPE + MLA_ROPE) ** -0.5 * LOG2E)
    full = lambda shape: pl.BlockSpec(shape, lambda bi, i: (0,) * len(shape))
    return pl.pallas_call(
        functools.partial(_qkv_up_body, qscale=qscale),
        grid=(b, ns),
        in_specs=[
            pl.BlockSpec((ts, MLA_Q_RANK), lambda bi, i: (row(bi, i), COL_CQ // MLA_Q_RANK)),
            pl.BlockSpec((ts, MLA_KV_RANK), lambda bi, i: (row(bi, i), COL_CKV // MLA_KV_RANK)),
            pl.BlockSpec((ts, 128), lambda bi, i: (row(bi, i), COL_KRA // 128)),
            pl.BlockSpec((ts, 128), lambda bi, i: (row(bi, i), COL_KRB // 128)),
            full((1, MLA_Q_RANK)),
            full((1, MLA_KV_RANK)),
            full((MLA_HEADS, MLA_QK_PAD, MLA_Q_RANK)),
            full((MLA_KV_RANK, MLA_HEADS * MLA_NOPE)),
            full((MLA_HEADS, MLA_V, MLA_KV_RANK)),
            pl.BlockSpec((MLA_ROPE, ts), lambda bi, i: (0, i)),
            pl.BlockSpec((MLA_ROPE, ts), lambda bi, i: (0, i)),
            pl.BlockSpec((ts, 128), lambda bi, i: (i, 0)),
            pl.BlockSpec((ts, 128), lambda bi, i: (i, 0)),
        ],
        out_specs=[
            pl.BlockSpec((1, MLA_HEADS, MLA_QK_PAD, ts), lambda bi, i: (bi, 0, 0, i)),
            pl.BlockSpec((1, MLA_HEADS, ts, MLA_QK_PAD), lambda bi, i: (bi, 0, i, 0)),
            pl.BlockSpec((1, MLA_HEADS, MLA_V, ts), lambda bi, i: (bi, 0, 0, i)),
        ],
        out_shape=[
            jax.ShapeDtypeStruct((b, MLA_HEADS, MLA_QK_PAD, s), BF16),
            jax.ShapeDtypeStruct((b, MLA_HEADS, s, MLA_QK_PAD), BF16),
            jax.ShapeDtypeStruct((b, MLA_HEADS, MLA_V, s), BF16),
        ],
        compiler_params=_cparams(("parallel", "parallel")),
        name="qkv_up",
    )(p, p, p, p, gq, gkv, wqt, wk, wvt, cq2, sq2, ck, sk)


def _mla_body(qt_ref, k_ref, vt_ref, o_ref, acc_ref, *, tq, tk):
    i = pl.program_id(2)
    qt = qt_ref[0, 0]
    acc_ref[...] = jnp.zeros_like(acc_ref)

    def chunk(c, carry, masked):
        m, l = carry
        k0 = pl.multiple_of(c * tk, tk)
        st = _dot(k_ref[0, 0, pl.ds(k0, tk), :], qt)
        if masked:
            kpos = k0 + lax.broadcasted_iota(jnp.int32, (tk, tq), 0)
            qpos = i * tq + lax.broadcasted_iota(jnp.int32, (tk, tq), 1)
            st = jnp.where(kpos <= qpos, st, NEG)
        m_new = jnp.maximum(m, jnp.max(st, axis=0, keepdims=True))
        alpha = jnp.exp2(m - m_new)
        p = jnp.exp2(st - m_new)
        l_new = alpha * l + jnp.sum(p, axis=0, keepdims=True)
        pv = _dot(vt_ref[0, 0, :, pl.ds(k0, tk)], p.astype(BF16))
        acc_ref[...] = alpha * acc_ref[...] + pv
        return m_new, l_new

    carry = (jnp.full((1, tq), NEG, F32), jnp.zeros((1, tq), F32))
    n_full = i * (tq // tk)
    carry = lax.fori_loop(0, n_full, lambda c, cr: chunk(c, cr, False), carry)
    for d in range(tq // tk):
        carry = chunk(n_full + d, carry, True)
    _, l = carry
    out_t = acc_ref[...] / l
    o_ref[0] = out_t.T.astype(o_ref.dtype)


def _mla_attention(qt, k, vt, tq, tk):
    b, h, _, s = qt.shape
    return pl.pallas_call(
        functools.partial(_mla_body, tq=tq, tk=tk),
        grid=(b, h, s // tq),
        in_specs=[
            pl.BlockSpec((1, 1, MLA_QK_PAD, tq), lambda bi, hi, i: (bi, hi, 0, i)),
            pl.BlockSpec((1, 1, s, MLA_QK_PAD), lambda bi, hi, i: (bi, hi, 0, 0)),
            pl.BlockSpec((1, 1, MLA_V, s), lambda bi, hi, i: (bi, hi, 0, 0)),
        ],
        out_specs=pl.BlockSpec((1, tq, MLA_V), lambda bi, hi, i: (bi, i, hi)),
        out_shape=jax.ShapeDtypeStruct((b, s, h * MLA_V), BF16),
        scratch_shapes=[pltpu.VMEM((MLA_V, tq), F32)],
        compiler_params=_cparams(("parallel", "parallel", "arbitrary")),
        name="mla_attn",
    )(qt, k, vt)


def _t5_bucket_table():
    n = np.arange(REL_MAX_DIST)
    max_exact = REL_BUCKETS // 2
    large = max_exact + (np.log(np.maximum(n, 1).astype(np.float32) / max_exact)
                         / math.log(REL_MAX_DIST / max_exact) * (REL_BUCKETS - max_exact)).astype(np.int32)
    large = np.minimum(large, REL_BUCKETS - 1)
    return np.where(n < max_exact, n, large)


def _swa_bias_body(bucket_ref, table_ref, o_ref):
    bucket = bucket_ref[...]
    kj = lax.broadcasted_iota(jnp.int32, bucket.shape, 1)
    for h in range(SWA_Q_HEADS):
        bias = jnp.zeros(bucket.shape, F32)
        for bk in range(REL_BUCKETS):
            bias = jnp.where(bucket == bk, table_ref[bk, h], bias)
        valid = bucket >= 0
        o_ref[1, h] = jnp.where(valid, bias, NEG)
        o_ref[0, h] = jnp.where(valid & (kj >= SWA_BLOCK), bias, NEG)


def _swa_bias(rel_table):
    qi = np.arange(SWA_BLOCK)[:, None]
    kj = np.arange(2 * SWA_BLOCK)[None, :]
    dist = qi + SWA_BLOCK - kj
    in_window = (dist >= 0) & (dist < SWA_BLOCK)
    bucket = np.where(in_window, _t5_bucket_table()[np.clip(dist, 0, REL_MAX_DIST - 1)], -1).astype(np.int32)
    return pl.pallas_call(
        _swa_bias_body,
        in_specs=[
            pl.BlockSpec(memory_space=pltpu.VMEM),
            pl.BlockSpec(memory_space=pltpu.SMEM),
        ],
        out_specs=pl.BlockSpec(memory_space=pltpu.VMEM),
        out_shape=jax.ShapeDtypeStruct((2, SWA_Q_HEADS, SWA_BLOCK, 2 * SWA_BLOCK), F32),
        name="swa_bias",
    )(jnp.asarray(bucket), rel_table)


def _swa_body(q_ref, kc_ref, kp_ref, vc_ref, vp_ref, bias_ref, sink_ref, o_ref):
    blk = SWA_BLOCK
    lane = lax.broadcasted_iota(jnp.int32, (2 * blk, 128), 1)
    lane_q = lax.broadcasted_iota(jnp.int32, (blk, 128), 1)
    kband = jnp.concatenate([kp_ref[...], kc_ref[...]], axis=0)
    vband = jnp.concatenate([vp_ref[...], vc_ref[...]], axis=0)

    def dup_half(x2, half):
        rolled = pltpu.roll(x2, 64, axis=1)
        first = lane < 64
        return jnp.where(first, x2, rolled) if half == 0 else jnp.where(first, rolled, x2)

    for kvh in range(SWA_KV_HEADS):
        pair_cols = slice((kvh // 2) * 128, (kvh // 2 + 1) * 128)
        k2 = dup_half(kband[:, pair_cols].astype(F32), kvh % 2).astype(BF16)
        v2 = dup_half(vband[:, pair_cols].astype(F32), kvh % 2).astype(BF16)
        parts = []
        for j in range(SWA_GROUP // 2):
            c0 = kvh * SWA_GROUP * SWA_DH + j * 128
            q2 = q_ref[:, c0:c0 + 128]
            zero = jnp.zeros_like(q2)
            parts.append(jnp.where(lane_q < 64, q2, zero))
            parts.append(jnp.where(lane_q < 64, zero, q2))
        qstack = jnp.concatenate(parts, axis=0)
        s_all = _dot_nt(qstack, k2) * (SWA_DH ** -0.5)
        ps = []
        dens = []
        for g in range(SWA_GROUP):
            hq = kvh * SWA_GROUP + g
            sg = s_all[g * blk:(g + 1) * blk] + bias_ref[0, hq]
            sink = sink_ref[hq]
            m = jnp.maximum(jnp.max(sg, axis=-1, keepdims=True), sink)
            p = jnp.exp(sg - m)
            dens.append(jnp.sum(p, axis=-1, keepdims=True) + jnp.exp(sink - m))
            ps.append(p.astype(BF16))
        o_all = _dot(jnp.concatenate(ps, axis=0), v2)
        for j in range(SWA_GROUP // 2):
            oa = o_all[(2 * j) * blk:(2 * j + 1) * blk] / dens[2 * j]
            ob = o_all[(2 * j + 1) * blk:(2 * j + 2) * blk] / dens[2 * j + 1]
            c0 = kvh * SWA_GROUP * SWA_DH + j * 128
            o_ref[:, c0:c0 + 128] = jnp.where(lane_q < 64, oa, ob).astype(o_ref.dtype)


def _swa_attention(p, bias, sinks, b, s):
    nb = s // SWA_BLOCK
    kvw = SWA_KV_HEADS * SWA_DH
    row = lambda bi, n: bi * nb + n
    prow = lambda bi, n: bi * nb + jnp.maximum(n - 1, 0)
    return pl.pallas_call(
        _swa_body,
        grid=(b, nb),
        in_specs=[
            pl.BlockSpec((SWA_BLOCK, D_MODEL), lambda bi, n: (row(bi, n), COL_QS // D_MODEL)),
            pl.BlockSpec((SWA_BLOCK, kvw), lambda bi, n: (row(bi, n), COL_KS // kvw)),
            pl.BlockSpec((SWA_BLOCK, kvw), lambda bi, n: (prow(bi, n), COL_KS // kvw)),
            pl.BlockSpec((SWA_BLOCK, kvw), lambda bi, n: (row(bi, n), COL_VS // kvw)),
            pl.BlockSpec((SWA_BLOCK, kvw), lambda bi, n: (prow(bi, n), COL_VS // kvw)),
            pl.BlockSpec((1, SWA_Q_HEADS, SWA_BLOCK, 2 * SWA_BLOCK),
                         lambda bi, n: (jnp.minimum(n, 1), 0, 0, 0)),
            pl.BlockSpec(memory_space=pltpu.SMEM),
        ],
        out_specs=pl.BlockSpec((SWA_BLOCK, D_MODEL), lambda bi, n: (row(bi, n), 0)),
        out_shape=jax.ShapeDtypeStruct((b * s, D_MODEL), BF16),
        compiler_params=_cparams(("parallel", "arbitrary")),
        name="swa_attn",
    )(p, p, p, p, p, bias, sinks)


def _merge_body(oa_ref, ob_ref, wa_ref, wb_ref, g0_ref, g1_ref, o_ref):
    ya = _dot(oa_ref[...], wa_ref[...])
    yb = _dot(ob_ref[...], wb_ref[...])
    g0 = jax.nn.sigmoid(g0_ref[...].astype(F32))
    g1 = jax.nn.sigmoid(g1_ref[...].astype(F32))
    o_ref[...] = (g0 * ya + g1 * yb).astype(o_ref.dtype)


def _merge(oa, ob, wa, wb, p, tm, tn):
    m = oa.shape[0]
    nj = D_MODEL // tn
    return pl.pallas_call(
        _merge_body,
        grid=(m // tm, nj),
        in_specs=[
            pl.BlockSpec((tm, D_MODEL), lambda i, j: (i, 0)),
            pl.BlockSpec((tm, D_MODEL), lambda i, j: (i, 0)),
            pl.BlockSpec((D_MODEL, tn), lambda i, j: (0, j)),
            pl.BlockSpec((D_MODEL, tn), lambda i, j: (0, j)),
            pl.BlockSpec((tm, tn), lambda i, j: (i, COL_GATES // tn + j)),
            pl.BlockSpec((tm, tn), lambda i, j: (i, COL_GATES // tn + nj + j)),
        ],
        out_specs=pl.BlockSpec((tm, tn), lambda i, j: (i, j)),
        out_shape=jax.ShapeDtypeStruct((m, D_MODEL), BF16),
        compiler_params=_cparams(("parallel", "arbitrary")),
        name="gated_merge",
    )(oa, ob, wa, wb, p, p)


def _out_proj_body(mg_ref, w_ref, x_ref, g_ref, o_ref):
    y = _dot(mg_ref[...], w_ref[...])
    o_ref[...] = x_ref[...] + _rms(y, g_ref[...])


def _out_proj(mg, w, x2, gain, tm):
    m = mg.shape[0]
    return pl.pallas_call(
        _out_proj_body,
        grid=(m // tm,),
        in_specs=[
            pl.BlockSpec((tm, D_MODEL), lambda i: (i, 0)),
            pl.BlockSpec((D_MODEL, D_MODEL), lambda i: (0, 0)),
            pl.BlockSpec((tm, D_MODEL), lambda i: (i, 0)),
            pl.BlockSpec((1, D_MODEL), lambda i: (0, 0)),
        ],
        out_specs=pl.BlockSpec((tm, D_MODEL), lambda i: (i, 0)),
        out_shape=jax.ShapeDtypeStruct((m, D_MODEL), F32),
        compiler_params=_cparams(("parallel",)),
        name="out_proj",
    )(mg, w, x2, gain)


HALO = 16


def _ffn_up_body(x_ref, xh_ref, g_ref, wa_ref, wb_ref, cw_ref, cb_ref, o_ref, hn_ref, a_ref, *, tiles_per_seq):
    i = pl.program_id(0)
    tm = x_ref.shape[0]

    @pl.when(pl.program_id(1) == 0)
    def _():
        halo = _rms(xh_ref[...], g_ref[...])
        halo = jnp.where(i % tiles_per_seq == 0, jnp.zeros_like(halo), halo)
        hn_ref[0:HALO, :] = halo.astype(BF16)
        hn_ref[HALO:, :] = _rms(x_ref[...], g_ref[...]).astype(BF16)

    a_ref[...] = _dot(hn_ref[...], wa_ref[...])
    gate = _dot(hn_ref[HALO:, :], wb_ref[...])
    c = cb_ref[...] + cw_ref[0:1, :] * a_ref[pl.ds(HALO - 2, tm), :]
    c = c + cw_ref[1:2, :] * a_ref[pl.ds(HALO - 1, tm), :]
    c = c + cw_ref[2:3, :] * a_ref[pl.ds(HALO, tm), :]
    o_ref[...] = (jax.nn.gelu(c, approximate=True) * gate).astype(o_ref.dtype)


def _ffn_up(x1, gain, w_up, conv_w, conv_b, s, tm, tn):
    m = x1.shape[0]
    nj = D_FF // tn
    hb = tm // HALO
    return pl.pallas_call(
        functools.partial(_ffn_up_body, tiles_per_seq=s // tm),
        grid=(m // tm, nj),
        in_specs=[
            pl.BlockSpec((tm, D_MODEL), lambda i, j: (i, 0)),
            pl.BlockSpec((HALO, D_MODEL), lambda i, j: (jnp.maximum(i * hb - 1, 0), 0)),
            pl.BlockSpec((1, D_MODEL), lambda i, j: (0, 0)),
            pl.BlockSpec((D_MODEL, tn), lambda i, j: (0, j)),
            pl.BlockSpec((D_MODEL, tn), lambda i, j: (0, nj + j)),
            pl.BlockSpec((3, tn), lambda i, j: (0, j)),
            pl.BlockSpec((1, tn), lambda i, j: (0, j)),
        ],
        out_specs=pl.BlockSpec((tm, tn), lambda i, j: (i, j)),
        out_shape=jax.ShapeDtypeStruct((m, D_FF), BF16),
        scratch_shapes=[pltpu.VMEM((HALO + tm, D_MODEL), BF16), pltpu.VMEM((HALO + tm, tn), F32)],
        compiler_params=_cparams(("parallel", "arbitrary")),
        name="ffn_up_conv",
    )(x1, x1, gain, w_up, w_up, conv_w, conv_b)


def _ffn_down_body(t_ref, w_ref, x_ref, g_ref, o_ref, acc_ref):
    kk = pl.program_id(1)

    @pl.when(kk == 0)
    def _():
        acc_ref[...] = jnp.zeros_like(acc_ref)

    acc_ref[...] += _dot(t_ref[...], w_ref[...])

    @pl.when(kk == pl.num_programs(1) - 1)
    def _():
        o_ref[...] = x_ref[...] + _rms(acc_ref[...], g_ref[...])


def _ffn_down(t, w, x1, gain, tm, tk):
    m = t.shape[0]
    return pl.pallas_call(
        _ffn_down_body,
        grid=(m // tm, D_FF // tk),
        in_specs=[
            pl.BlockSpec((tm, tk), lambda i, k: (i, k)),
            pl.BlockSpec((tk, D_MODEL), lambda i, k: (k, 0)),
            pl.BlockSpec((tm, D_MODEL), lambda i, k: (i, 0)),
            pl.BlockSpec((1, D_MODEL), lambda i, k: (0, 0)),
        ],
        out_specs=pl.BlockSpec((tm, D_MODEL), lambda i, k: (i, 0)),
        out_shape=jax.ShapeDtypeStruct((m, D_MODEL), F32),
        scratch_shapes=[pltpu.VMEM((tm, D_MODEL), F32)],
        compiler_params=_cparams(("parallel", "arbitrary")),
        name="ffn_down",
    )(t, w, x1, gain)


def _prep_w_in(w_in):
    cq = w_in[:, 0:512]
    ckv = w_in[:, 512:768]
    kr = w_in[:, 768:832]
    qs = w_in[:, 832:2880]
    ks = w_in[:, 2880:3136]
    vs = w_in[:, 3136:3392]
    gates = w_in[:, 3392:7488]
    z64 = jnp.zeros((D_MODEL, 64), w_in.dtype)
    kr_sw = jnp.concatenate([kr[:, 32:], kr[:, :32]], axis=1)
    w = jnp.concatenate([gates, qs, cq, ks, vs, ckv, kr, z64, kr_sw, z64], axis=1)
    return w.astype(BF16)


def _prep_w_q_up(w):
    w = w.reshape(MLA_Q_RANK, MLA_HEADS, MLA_NOPE + MLA_ROPE)
    nope = w[:, :, :MLA_NOPE]
    rope = w[:, :, MLA_NOPE:]
    rope_sw = jnp.concatenate([rope[:, :, 32:], rope[:, :, :32]], axis=2)
    wq = jnp.concatenate([nope, rope, rope_sw], axis=2)
    return jnp.transpose(wq, (1, 2, 0)).astype(BF16)


def _prep_w_kv_up(w):
    w = w.reshape(MLA_KV_RANK, MLA_HEADS, MLA_NOPE + MLA_V)
    wk = w[:, :, :MLA_NOPE].reshape(MLA_KV_RANK, MLA_HEADS * MLA_NOPE).astype(BF16)
    wvt = jnp.transpose(w[:, :, MLA_NOPE:], (1, 2, 0)).astype(BF16)
    return wk, wvt


def _rope_tables(s):
    inv = ROPE_THETA ** (-jnp.arange(0, MLA_ROPE, 2, dtype=F32) / MLA_ROPE)
    ang = jnp.arange(s).astype(F32)[:, None] * inv[None, :]
    cos, sin = jnp.cos(ang), jnp.sin(ang)
    z = jnp.zeros((s, 64), F32)
    ck = jnp.concatenate([cos, cos, z], axis=1)
    sk = jnp.concatenate([-sin, sin, z], axis=1)
    cq2 = jnp.concatenate([cos, cos], axis=1).T
    sq2 = jnp.concatenate([-sin, sin], axis=1).T
    return cq2, sq2, ck, sk


def _tile(n, pref):
    t = min(n, pref)
    assert n % t == 0, (n, pref)
    return t


def kernel(x, norm_mix_pre, norm_mix_post, norm_ffn_pre, norm_ffn_post, w_in, mla_q_norm, mla_w_q_up, mla_kv_norm, mla_w_kv_up, swa_sinks, rel_bias_table, w_o_mla, w_o_swa, w_out, ffn_w_up, ffn_conv_w, ffn_conv_b, ffn_w_down):
    b, s, d = x.shape
    assert d == D_MODEL and s % SWA_BLOCK == 0
    depth = w_in.shape[0]
    m = b * s
    x2 = x.reshape(m, d)
    cq2, sq2, ck, sk = _rope_tables(s)
    bias = _swa_bias(rel_bias_table.astype(F32))
    for l in range(depth):
        p = _in_proj(x2, norm_mix_pre[l][None], _prep_w_in(w_in[l]), _tile(m, 1024), 1280)
        wk, wvt = _prep_w_kv_up(mla_w_kv_up[l])
        qt, k, vt = _qkv_up(p, mla_q_norm[l][None], mla_kv_norm[l][None], _prep_w_q_up(mla_w_q_up[l]),
                            wk, wvt, cq2, sq2, ck, sk, b, s, _tile(s, 512))
        o_a = _mla_attention(qt, k, vt, _tile(s, 512), _tile(s, 512)).reshape(m, d)
        o_b = _swa_attention(p, bias, swa_sinks[l].astype(F32), b, s)
        mg = _merge(o_a, o_b, w_o_mla[l].astype(BF16), w_o_swa[l].astype(BF16), p, _tile(m, 512), 1024)
        x1 = _out_proj(mg, w_out[l].astype(BF16), x2, norm_mix_post[l][None], _tile(m, 512))
        t = _ffn_up(x1, norm_ffn_pre[l][None], ffn_w_up[l].astype(BF16), ffn_conv_w[l], ffn_conv_b[l][None],
                    s, _tile(s, 512), 1408)
        x2 = _ffn_down(t, ffn_w_down[l].astype(BF16), x1, norm_ffn_post[l][None], _tile(m, 512), 1408)
    return x2.reshape(b, s, d)
```

```python
import functools
import math

import numpy as np
import jax
import jax.numpy as jnp
from jax import lax
from jax.experimental import pallas as pl
from jax.experimental.pallas import tpu as pltpu

F32 = jnp.float32
BF16 = jnp.bfloat16

D_MODEL = 2048
MLA_HEADS = 16
MLA_Q_RANK = 512
MLA_KV_RANK = 256
MLA_NOPE = 128
MLA_ROPE = 64
MLA_V = 128
MLA_QK_PAD = 256
ROPE_THETA = 10000.0
SWA_Q_HEADS = 32
SWA_KV_HEADS = 4
SWA_GROUP = 8
SWA_DH = 64
SWA_BLOCK = 128
REL_BUCKETS = 32
REL_MAX_DIST = 128
D_FF = 5632
EPS = 1e-6
NEG = -1e30
LOG2E = 1.4426950408889634

COL_GATES = 0
COL_QS = 4096
COL_CQ = 6144
COL_KS = 6656
COL_VS = 6912
COL_CKV = 7168
COL_KRA = 7424
COL_KRB = 7552
IN_COLS_PAD = 7680

VMEM_LIMIT = 56 * 1024 * 1024


def _cparams(sem):
    return pltpu.CompilerParams(dimension_semantics=sem, vmem_limit_bytes=VMEM_LIMIT)


def _rms(x, g):
    return x * lax.rsqrt(jnp.mean(x * x, axis=-1, keepdims=True) + EPS) * g


def _dot(a, b):
    return jnp.dot(a, b, preferred_element_type=F32)


def _dot_nt(a, b):
    return lax.dot_general(a, b, (((1,), (1,)), ((), ())), preferred_element_type=F32)


def _in_proj_body(x_ref, g_ref, w_ref, o_ref, hn_ref):
    @pl.when(pl.program_id(1) == 0)
    def _():
        hn_ref[...] = _rms(x_ref[...], g_ref[...]).astype(BF16)

    o_ref[...] = _dot(hn_ref[...], w_ref[...]).astype(o_ref.dtype)


def _in_proj(x2, gain, w, tm, tn):
    m = x2.shape[0]
    n = w.shape[1]
    return pl.pallas_call(
        _in_proj_body,
        grid=(m // tm, n // tn),
        in_specs=[
            pl.BlockSpec((tm, D_MODEL), lambda i, j: (i, 0)),
            pl.BlockSpec((1, D_MODEL), lambda i, j: (0, 0)),
            pl.BlockSpec((D_MODEL, tn), lambda i, j: (0, j)),
        ],
        out_specs=pl.BlockSpec((tm, tn), lambda i, j: (i, j)),
        out_shape=jax.ShapeDtypeStruct((m, n), BF16),
        scratch_shapes=[pltpu.VMEM((tm, D_MODEL), BF16)],
        compiler_params=_cparams(("parallel", "arbitrary")),
        name="in_proj",
    )(x2, gain, w)


def _qkv_up_body(cq_ref, ckv_ref, kra_ref, krb_ref, gq_ref, gkv_ref, wqt_ref, wk_ref, wvt_ref,
                 cq2_ref, sq2_ref, ck_ref, sk_ref, qt_ref, k_ref, vt_ref, *, qscale):
    ts = cq_ref.shape[0]
    cqn = _rms(cq_ref[...].astype(F32), gq_ref[...]).astype(BF16)
    ckvn = _rms(ckv_ref[...].astype(F32), gkv_ref[...]).astype(BF16)
    cq2 = cq2_ref[...]
    sq2 = sq2_ref[...]
    zeros_q = jnp.zeros((MLA_QK_PAD - MLA_NOPE - MLA_ROPE, ts), BF16)
    for h in range(MLA_HEADS):
        qf = _dot_nt(wqt_ref[h], cqn) * qscale
        rope = qf[MLA_NOPE:MLA_NOPE + MLA_ROPE] * cq2 + qf[MLA_NOPE + MLA_ROPE:] * sq2
        qt_ref[0, h, 0:MLA_NOPE, :] = qf[0:MLA_NOPE].astype(BF16)
        qt_ref[0, h, MLA_NOPE:MLA_NOPE + MLA_ROPE, :] = rope.astype(BF16)
        qt_ref[0, h, MLA_NOPE + MLA_ROPE:, :] = zeros_q
    kr = (kra_ref[...].astype(F32) * ck_ref[...] + krb_ref[...].astype(F32) * sk_ref[...]).astype(BF16)
    kall = _dot(ckvn, wk_ref[...]).astype(BF16)
    for h in range(MLA_HEADS):
        k_ref[0, h, :, 0:MLA_NOPE] = kall[:, h * MLA_NOPE:(h + 1) * MLA_NOPE]
        k_ref[0, h, :, MLA_NOPE:] = kr
        vt_ref[0, h] = _dot_nt(wvt_ref[h], ckvn).astype(BF16)


def _qkv_up(p, gq, gkv, wqt, wk, wvt, cq2, sq2, ck, sk, b, s, ts):
    ns = s // ts
    row = lambda bi, i: bi * ns + i
    qscale = float((MLA_NOPE + MLA_ROPE) ** -0.5 * LOG2E)
    full = lambda shape: pl.BlockSpec(shape, lambda bi, i: (0,) * len(shape))
    return pl.pallas_call(
        functools.partial(_qkv_up_body, qscale=qscale),
        grid=(b, ns),
        in_specs=[
            pl.BlockSpec((ts, MLA_Q_RANK), lambda bi, i: (row(bi, i), COL_CQ // MLA_Q_RANK)),
            pl.BlockSpec((ts, MLA_KV_RANK), lambda bi, i: (row(bi, i), COL_CKV // MLA_KV_RANK)),
            pl.BlockSpec((ts, 128), lambda bi, i: (row(bi, i), COL_KRA // 128)),
            pl.BlockSpec((ts, 128), lambda bi, i: (row(bi, i), COL_KRB // 128)),
            full((1, MLA_Q_RANK)),
            full((1, MLA_KV_RANK)),
            full((MLA_HEADS, MLA_QK_PAD, MLA_Q_RANK)),
            full((MLA_KV_RANK, MLA_HEADS * MLA_NOPE)),
            full((MLA_HEADS, MLA_V, MLA_KV_RANK)),
            pl.BlockSpec((MLA_ROPE, ts), lambda bi, i: (0, i)),
            pl.BlockSpec((MLA_ROPE, ts), lambda bi, i: (0, i)),
            pl.BlockSpec((ts, 128), lambda bi, i: (i, 0)),
            pl.BlockSpec((ts, 128), lambda bi, i: (i, 0)),
        ],
        out_specs=[
            pl.BlockSpec((1, MLA_HEADS, MLA_QK_PAD, ts), lambda bi, i: (bi, 0, 0, i)),
            pl.BlockSpec((1, MLA_HEADS, ts, MLA_QK_PAD), lambda bi, i: (bi, 0, i, 0)),
            pl.BlockSpec((1, MLA_HEADS, MLA_V, ts), lambda bi, i: (bi, 0, 0, i)),
        ],
        out_shape=[
            jax.ShapeDtypeStruct((b, MLA_HEADS, MLA_QK_PAD, s), BF16),
            jax.ShapeDtypeStruct((b, MLA_HEADS, s, MLA_QK_PAD), BF16),
            jax.ShapeDtypeStruct((b, MLA_HEADS, MLA_V, s), BF16),
        ],
        compiler_params=_cparams(("parallel", "parallel")),
        name="qkv_up",
    )(p, p, p, p, gq, gkv, wqt, wk, wvt, cq2, sq2, ck, sk)


def _mla_body(qt_ref, k_ref, vt_ref, o_ref, acc_ref, m_ref, l_ref,
              st0, st1, p0, p1, mc0, mc1, al0, al1, *, tq, tk):
    i = pl.program_id(2)
    st_s, p_s, mc_s, al_s = (st0, st1), (p0, p1), (mc0, mc1), (al0, al1)
    q_base = i * tq

    acc_ref[...] = jnp.zeros_like(acc_ref)
    m_ref[...] = jnp.full_like(m_ref, NEG)
    l_ref[...] = jnp.zeros_like(l_ref)

    def diag_block(key_off, col_off, masked):
        cols = slice(col_off, col_off + tk)
        k0 = pl.multiple_of(q_base + key_off, tk)
        st = _dot(k_ref[0, 0, pl.ds(k0, tk), :], qt_ref[0, 0, :, cols])
        if masked:
            kpos = lax.broadcasted_iota(jnp.int32, (tk, tk), 0)
            qpos = lax.broadcasted_iota(jnp.int32, (tk, tk), 1)
            st = jnp.where(kpos <= qpos, st, NEG)
        m_old = m_ref[:, cols]
        m_new = jnp.maximum(m_old, jnp.max(st, axis=0, keepdims=True))
        alpha = jnp.exp2(m_old - m_new)
        p = jnp.exp2(st - m_new)
        l_ref[:, cols] = alpha * l_ref[:, cols] + jnp.sum(p, axis=0, keepdims=True)
        m_ref[:, cols] = m_new
        pv = _dot(vt_ref[0, 0, :, pl.ds(k0, tk)], p.astype(BF16))
        acc_ref[:, cols] = alpha * acc_ref[:, cols] + pv

    for kb in range(tq // tk):
        for qb in range(kb, tq // tk):
            diag_block(kb * tk, qb * tk, masked=(kb == qb))

    def qk(c, slot):
        k0 = pl.multiple_of(c * tk, tk)
        st = _dot(k_ref[0, 0, pl.ds(k0, tk), :], qt_ref[0, 0])
        st_s[slot][...] = st
        mc_s[slot][...] = jnp.max(st, axis=0, keepdims=True)

    def softmax(slot):
        m_old = m_ref[...]
        m_new = jnp.maximum(m_old, mc_s[slot][...])
        alpha = jnp.exp2(m_old - m_new)
        p = jnp.exp2(st_s[slot][...] - m_new)
        l_ref[...] = alpha * l_ref[...] + jnp.sum(p, axis=0, keepdims=True)
        m_ref[...] = m_new
        al_s[slot][...] = alpha
        p_s[slot][...] = p.astype(BF16)

    def pv(c, slot):
        k0 = pl.multiple_of(c * tk, tk)
        acc_ref[...] = al_s[slot][...] * acc_ref[...] + _dot(vt_ref[0, 0, :, pl.ds(k0, tk)], p_s[slot][...])

    n_full = i * (tq // tk)

    @pl.when(i > 0)
    def _():
        qk(0, 0)
        qk(1, 1)
        softmax(0)

        def pair(u, carry):
            t = 2 * u + 1
            qk(t + 1, 0)
            softmax(1)
            pv(t - 1, 0)
            qk(t + 2, 1)
            softmax(0)
            pv(t, 1)
            return carry

        lax.fori_loop(0, i - 1, pair, 0)
        softmax(1)
        pv(n_full - 2, 0)
        pv(n_full - 1, 1)

    out_t = acc_ref[...] / l_ref[...]
    o_ref[0] = out_t.T.astype(o_ref.dtype)


def _mla_attention(qt, k, vt, tq, tk):
    b, h, _, s = qt.shape
    assert tq == 2 * tk
    return pl.pallas_call(
        functools.partial(_mla_body, tq=tq, tk=tk),
        grid=(b, h, s // tq),
        in_specs=[
            pl.BlockSpec((1, 1, MLA_QK_PAD, tq), lambda bi, hi, i: (bi, hi, 0, i)),
            pl.BlockSpec((1, 1, s, MLA_QK_PAD), lambda bi, hi, i: (bi, hi, 0, 0)),
            pl.BlockSpec((1, 1, MLA_V, s), lambda bi, hi, i: (bi, hi, 0, 0)),
        ],
        out_specs=pl.BlockSpec((1, tq, MLA_V), lambda bi, hi, i: (bi, i, hi)),
        out_shape=jax.ShapeDtypeStruct((b, s, h * MLA_V), BF16),
        scratch_shapes=[
            pltpu.VMEM((MLA_V, tq), F32),
            pltpu.VMEM((1, tq), F32),
            pltpu.VMEM((1, tq), F32),
            pltpu.VMEM((tk, tq), F32), pltpu.VMEM((tk, tq), F32),
            pltpu.VMEM((tk, tq), BF16), pltpu.VMEM((tk, tq), BF16),
            pltpu.VMEM((1, tq), F32), pltpu.VMEM((1, tq), F32),
            pltpu.VMEM((1, tq), F32), pltpu.VMEM((1, tq), F32),
        ],
        compiler_params=_cparams(("parallel", "parallel", "arbitrary")),
        name="mla_attn",
    )(qt, k, vt)


def _t5_bucket_table():
    n = np.arange(REL_MAX_DIST)
    max_exact = REL_BUCKETS // 2
    large = max_exact + (np.log(np.maximum(n, 1).astype(np.float32) / max_exact)
                         / math.log(REL_MAX_DIST / max_exact) * (REL_BUCKETS - max_exact)).astype(np.int32)
    large = np.minimum(large, REL_BUCKETS - 1)
    return np.where(n < max_exact, n, large)


def _swa_bias_body(bucket_ref, table_ref, o_ref):
    bucket = bucket_ref[...]
    kj = lax.broadcasted_iota(jnp.int32, bucket.shape, 1)
    for h in range(SWA_Q_HEADS):
        bias = jnp.zeros(bucket.shape, F32)
        for bk in range(REL_BUCKETS):
            bias = jnp.where(bucket == bk, table_ref[bk, h], bias)
        valid = bucket >= 0
        o_ref[1, h] = jnp.where(valid, bias, NEG)
        o_ref[0, h] = jnp.where(valid & (kj >= SWA_BLOCK), bias, NEG)


def _swa_bias(rel_table):
    qi = np.arange(SWA_BLOCK)[:, None]
    kj = np.arange(2 * SWA_BLOCK)[None, :]
    dist = qi + SWA_BLOCK - kj
    in_window = (dist >= 0) & (dist < SWA_BLOCK)
    bucket = np.where(in_window, _t5_bucket_table()[np.clip(dist, 0, REL_MAX_DIST - 1)], -1).astype(np.int32)
    return pl.pallas_call(
        _swa_bias_body,
        in_specs=[
            pl.BlockSpec(memory_space=pltpu.VMEM),
            pl.BlockSpec(memory_space=pltpu.SMEM),
        ],
        out_specs=pl.BlockSpec(memory_space=pltpu.VMEM),
        out_shape=jax.ShapeDtypeStruct((2, SWA_Q_HEADS, SWA_BLOCK, 2 * SWA_BLOCK), F32),
        name="swa_bias",
    )(jnp.asarray(bucket), rel_table)


def _swa_body(q_ref, kc_ref, kp_ref, vc_ref, vp_ref, bias_ref, sink_ref, o_ref):
    blk = SWA_BLOCK
    lane = lax.broadcasted_iota(jnp.int32, (2 * blk, 128), 1)
    lane_q = lax.broadcasted_iota(jnp.int32, (blk, 128), 1)
    kband = jnp.concatenate([kp_ref[...], kc_ref[...]], axis=0)
    vband = jnp.concatenate([vp_ref[...], vc_ref[...]], axis=0)

    def dup_half(x2, half):
        rolled = pltpu.roll(x2, 64, axis=1)
        first = lane < 64
        return jnp.where(first, x2, rolled) if half == 0 else jnp.where(first, rolled, x2)

    for kvh in range(SWA_KV_HEADS):
        pair_cols = slice((kvh // 2) * 128, (kvh // 2 + 1) * 128)
        k2 = dup_half(kband[:, pair_cols].astype(F32), kvh % 2).astype(BF16)
        v2 = dup_half(vband[:, pair_cols].astype(F32), kvh % 2).astype(BF16)
        parts = []
        for j in range(SWA_GROUP // 2):
            c0 = kvh * SWA_GROUP * SWA_DH + j * 128
            q2 = q_ref[:, c0:c0 + 128]
            zero = jnp.zeros_like(q2)
            parts.append(jnp.where(lane_q < 64, q2, zero))
            parts.append(jnp.where(lane_q < 64, zero, q2))
        qstack = jnp.concatenate(parts, axis=0)
        s_all = _dot_nt(qstack, k2) * (SWA_DH ** -0.5)
        ps = []
        dens = []
        for g in range(SWA_GROUP):
            hq = kvh * SWA_GROUP + g
            sg = s_all[g * blk:(g + 1) * blk] + bias_ref[0, hq]
            sink = sink_ref[hq]
            m = jnp.maximum(jnp.max(sg, axis=-1, keepdims=True), sink)
            p = jnp.exp(sg - m)
            dens.append(jnp.sum(p, axis=-1, keepdims=True) + jnp.exp(sink - m))
            ps.append(p.astype(BF16))
        o_all = _dot(jnp.concatenate(ps, axis=0), v2)
        for j in range(SWA_GROUP // 2):
            oa = o_all[(2 * j) * blk:(2 * j + 1) * blk] / dens[2 * j]
            ob = o_all[(2 * j + 1) * blk:(2 * j + 2) * blk] / dens[2 * j + 1]
            c0 = kvh * SWA_GROUP * SWA_DH + j * 128
            o_ref[:, c0:c0 + 128] = jnp.where(lane_q < 64, oa, ob).astype(o_ref.dtype)


def _swa_attention(p, bias, sinks, b, s):
    nb = s // SWA_BLOCK
    kvw = SWA_KV_HEADS * SWA_DH
    row = lambda bi, n: bi * nb + n
    prow = lambda bi, n: bi * nb + jnp.maximum(n - 1, 0)
    return pl.pallas_call(
        _swa_body,
        grid=(b, nb),
        in_specs=[
            pl.BlockSpec((SWA_BLOCK, D_MODEL), lambda bi, n: (row(bi, n), COL_QS // D_MODEL)),
            pl.BlockSpec((SWA_BLOCK, kvw), lambda bi, n: (row(bi, n), COL_KS // kvw)),
            pl.BlockSpec((SWA_BLOCK, kvw), lambda bi, n: (prow(bi, n), COL_KS // kvw)),
            pl.BlockSpec((SWA_BLOCK, kvw), lambda bi, n: (row(bi, n), COL_VS // kvw)),
            pl.BlockSpec((SWA_BLOCK, kvw), lambda bi, n: (prow(bi, n), COL_VS // kvw)),
            pl.BlockSpec((1, SWA_Q_HEADS, SWA_BLOCK, 2 * SWA_BLOCK),
                         lambda bi, n: (jnp.minimum(n, 1), 0, 0, 0)),
            pl.BlockSpec(memory_space=pltpu.SMEM),
        ],
        out_specs=pl.BlockSpec((SWA_BLOCK, D_MODEL), lambda bi, n: (row(bi, n), 0)),
        out_shape=jax.ShapeDtypeStruct((b * s, D_MODEL), BF16),
        compiler_params=_cparams(("parallel", "arbitrary")),
        name="swa_attn",
    )(p, p, p, p, p, bias, sinks)


def _merge_body(oa_ref, ob_ref, wa_ref, wb_ref, g0_ref, g1_ref, o_ref):
    ya = _dot(oa_ref[...], wa_ref[...])
    yb = _dot(ob_ref[...], wb_ref[...])
    g0 = jax.nn.sigmoid(g0_ref[...].astype(F32))
    g1 = jax.nn.sigmoid(g1_ref[...].astype(F32))
    o_ref[...] = (g0 * ya + g1 * yb).astype(o_ref.dtype)


def _merge(oa, ob, wa, wb, p, tm, tn):
    m = oa.shape[0]
    nj = D_MODEL // tn
    return pl.pallas_call(
        _merge_body,
        grid=(m // tm, nj),
        in_specs=[
            pl.BlockSpec((tm, D_MODEL), lambda i, j: (i, 0)),
            pl.BlockSpec((tm, D_MODEL), lambda i, j: (i, 0)),
            pl.BlockSpec((D_MODEL, tn), lambda i, j: (0, j)),
            pl.BlockSpec((D_MODEL, tn), lambda i, j: (0, j)),
            pl.BlockSpec((tm, tn), lambda i, j: (i, COL_GATES // tn + j)),
            pl.BlockSpec((tm, tn), lambda i, j: (i, COL_GATES // tn + nj + j)),
        ],
        out_specs=pl.BlockSpec((tm, tn), lambda i, j: (i, j)),
        out_shape=jax.ShapeDtypeStruct((m, D_MODEL), BF16),
        compiler_params=_cparams(("parallel", "arbitrary")),
        name="gated_merge",
    )(oa, ob, wa, wb, p, p)


def _out_proj_body(mg_ref, w_ref, x_ref, g_ref, o_ref):
    y = _dot(mg_ref[...], w_ref[...])
    o_ref[...] = x_ref[...] + _rms(y, g_ref[...])


def _out_proj(mg, w, x2, gain, tm):
    m = mg.shape[0]
    return pl.pallas_call(
        _out_proj_body,
        grid=(m // tm,),
        in_specs=[
            pl.BlockSpec((tm, D_MODEL), lambda i: (i, 0)),
            pl.BlockSpec((D_MODEL, D_MODEL), lambda i: (0, 0)),
            pl.BlockSpec((tm, D_MODEL), lambda i: (i, 0)),
            pl.BlockSpec((1, D_MODEL), lambda i: (0, 0)),
        ],
        out_specs=pl.BlockSpec((tm, D_MODEL), lambda i: (i, 0)),
        out_shape=jax.ShapeDtypeStruct((m, D_MODEL), F32),
        compiler_params=_cparams(("parallel",)),
        name="out_proj",
    )(mg, w, x2, gain)


HALO = 16


def _ffn_up_body(x_ref, xh_ref, g_ref, wa_ref, wb_ref, cw_ref, cb_ref, o_ref, hn_ref, a_ref, *, tiles_per_seq):
    i = pl.program_id(0)
    tm = x_ref.shape[0]

    @pl.when(pl.program_id(1) == 0)
    def _():
        halo = _rms(xh_ref[...], g_ref[...])
        halo = jnp.where(i % tiles_per_seq == 0, jnp.zeros_like(halo), halo)
        hn_ref[0:HALO, :] = halo.astype(BF16)
        hn_ref[HALO:, :] = _rms(x_ref[...], g_ref[...]).astype(BF16)

    a_ref[...] = _dot(hn_ref[...], wa_ref[...])
    gate = _dot(hn_ref[HALO:, :], wb_ref[...])
    c = cb_ref[...] + cw_ref[0:1, :] * a_ref[pl.ds(HALO - 2, tm), :]
    c = c + cw_ref[1:2, :] * a_ref[pl.ds(HALO - 1, tm), :]
    c = c + cw_ref[2:3, :] * a_ref[pl.ds(HALO, tm), :]
    o_ref[...] = (jax.nn.gelu(c, approximate=True) * gate).astype(o_ref.dtype)


def _ffn_up(x1, gain, w_up, conv_w, conv_b, s, tm, tn):
    m = x1.shape[0]
    nj = D_FF // tn
    hb = tm // HALO
    return pl.pallas_call(
        functools.partial(_ffn_up_body, tiles_per_seq=s // tm),
        grid=(m // tm, nj),
        in_specs=[
            pl.BlockSpec((tm, D_MODEL), lambda i, j: (i, 0)),
            pl.BlockSpec((HALO, D_MODEL), lambda i, j: (jnp.maximum(i * hb - 1, 0), 0)),
            pl.BlockSpec((1, D_MODEL), lambda i, j: (0, 0)),
            pl.BlockSpec((D_MODEL, tn), lambda i, j: (0, j)),
            pl.BlockSpec((D_MODEL, tn), lambda i, j: (0, nj + j)),
            pl.BlockSpec((3, tn), lambda i, j: (0, j)),
            pl.BlockSpec((1, tn), lambda i, j: (0, j)),
        ],
        out_specs=pl.BlockSpec((tm, tn), lambda i, j: (i, j)),
        out_shape=jax.ShapeDtypeStruct((m, D_FF), BF16),
        scratch_shapes=[pltpu.VMEM((HALO + tm, D_MODEL), BF16), pltpu.VMEM((HALO + tm, tn), F32)],
        compiler_params=_cparams(("parallel", "arbitrary")),
        name="ffn_up_conv",
    )(x1, x1, gain, w_up, w_up, conv_w, conv_b)


def _ffn_down_body(t_ref, w_ref, x_ref, g_ref, o_ref, acc_ref):
    kk = pl.program_id(1)

    @pl.when(kk == 0)
    def _():
        acc_ref[...] = jnp.zeros_like(acc_ref)

    acc_ref[...] += _dot(t_ref[...], w_ref[...])

    @pl.when(kk == pl.num_programs(1) - 1)
    def _():
        o_ref[...] = x_ref[...] + _rms(acc_ref[...], g_ref[...])


def _ffn_down(t, w, x1, gain, tm, tk):
    m = t.shape[0]
    return pl.pallas_call(
        _ffn_down_body,
        grid=(m // tm, D_FF // tk),
        in_specs=[
            pl.BlockSpec((tm, tk), lambda i, k: (i, k)),
            pl.BlockSpec((tk, D_MODEL), lambda i, k: (k, 0)),
            pl.BlockSpec((tm, D_MODEL), lambda i, k: (i, 0)),
            pl.BlockSpec((1, D_MODEL), lambda i, k: (0, 0)),
        ],
        out_specs=pl.BlockSpec((tm, D_MODEL), lambda i, k: (i, 0)),
        out_shape=jax.ShapeDtypeStruct((m, D_MODEL), F32),
        scratch_shapes=[pltpu.VMEM((tm, D_MODEL), F32)],
        compiler_params=_cparams(("parallel", "arbitrary")),
        name="ffn_down",
    )(t, w, x1, gain)


def _prep_w_in(w_in):
    cq = w_in[:, 0:512]
    ckv = w_in[:, 512:768]
    kr = w_in[:, 768:832]
    qs = w_in[:, 832:2880]
    ks = w_in[:, 2880:3136]
    vs = w_in[:, 3136:3392]
    gates = w_in[:, 3392:7488]
    z64 = jnp.zeros((D_MODEL, 64), w_in.dtype)
    kr_sw = jnp.concatenate([kr[:, 32:], kr[:, :32]], axis=1)
    w = jnp.concatenate([gates, qs, cq, ks, vs, ckv, kr, z64, kr_sw, z64], axis=1)
    return w.astype(BF16)


def _prep_w_q_up(w):
    w = w.reshape(MLA_Q_RANK, MLA_HEADS, MLA_NOPE + MLA_ROPE)
    nope = w[:, :, :MLA_NOPE]
    rope = w[:, :, MLA_NOPE:]
    rope_sw = jnp.concatenate([rope[:, :, 32:], rope[:, :, :32]], axis=2)
    wq = jnp.concatenate([nope, rope, rope_sw], axis=2)
    return jnp.transpose(wq, (1, 2, 0)).astype(BF16)


def _prep_w_kv_up(w):
    w = w.reshape(MLA_KV_RANK, MLA_HEADS, MLA_NOPE + MLA_V)
    wk = w[:, :, :MLA_NOPE].reshape(MLA_KV_RANK, MLA_HEADS * MLA_NOPE).astype(BF16)
    wvt = jnp.transpose(w[:, :, MLA_NOPE:], (1, 2, 0)).astype(BF16)
    return wk, wvt


def _rope_tables(s):
    inv = ROPE_THETA ** (-jnp.arange(0, MLA_ROPE, 2, dtype=F32) / MLA_ROPE)
    ang = jnp.arange(s).astype(F32)[:, None] * inv[None, :]
    cos, sin = jnp.cos(ang), jnp.sin(ang)
    z = jnp.zeros((s, 64), F32)
    ck = jnp.concatenate([cos, cos, z], axis=1)
    sk = jnp.concatenate([-sin, sin, z], axis=1)
    cq2 = jnp.concatenate([cos, cos], axis=1).T
    sq2 = jnp.concatenate([-sin, sin], axis=1).T
    return cq2, sq2, ck, sk


def _tile(n, pref):
    t = min(n, pref)
    assert n % t == 0, (n, pref)
    return t


def kernel(x, norm_mix_pre, norm_mix_post, norm_ffn_pre, norm_ffn_post, w_in, mla_q_norm, mla_w_q_up, mla_kv_norm, mla_w_kv_up, swa_sinks, rel_bias_table, w_o_mla, w_o_swa, w_out, ffn_w_up, ffn_conv_w, ffn_conv_b, ffn_w_down):
    b, s, d = x.shape
    assert d == D_MODEL and s % SWA_BLOCK == 0
    depth = w_in.shape[0]
    m = b * s
    x2 = x.reshape(m, d)
    cq2, sq2, ck, sk = _rope_tables(s)
    bias = _swa_bias(rel_bias_table.astype(F32))
    for l in range(depth):
        p = _in_proj(x2, norm_mix_pre[l][None], _prep_w_in(w_in[l]), _tile(m, 1024), 1280)
        wk, wvt = _prep_w_kv_up(mla_w_kv_up[l])
        qt, k, vt = _qkv_up(p, mla_q_norm[l][None], mla_kv_norm[l][None], _prep_w_q_up(mla_w_q_up[l]),
                            wk, wvt, cq2, sq2, ck, sk, b, s, _tile(s, 512))
        tq = _tile(s, 1024)
        o_a = _mla_attention(qt, k, vt, tq, tq // 2).reshape(m, d)
        o_b = _swa_attention(p, bias, swa_sinks[l].astype(F32), b, s)
        mg = _merge(o_a, o_b, w_o_mla[l].astype(BF16), w_o_swa[l].astype(BF16), p, _tile(m, 512), 1024)
        x1 = _out_proj(mg, w_out[l].astype(BF16), x2, norm_mix_post[l][None], _tile(m, 512))
        t = _ffn_up(x1, norm_ffn_pre[l][None], ffn_w_up[l].astype(BF16), ffn_conv_w[l], ffn_conv_b[l][None],
                    s, _tile(s, 512), 1408)
        x2 = _ffn_down(t, ffn_w_down[l].astype(BF16), x1, norm_ffn_post[l][None], _tile(m, 512), 1408)
    return x2.reshape(b, s, d)
```

```python
import functools
import math

import numpy as np
import jax
import jax.numpy as jnp
from jax import lax
from jax.experimental import pallas as pl
from jax.experimental.pallas import tpu as pltpu

F32 = jnp.float32
BF16 = jnp.bfloat16

D_MODEL = 2048
MLA_HEADS = 16
MLA_Q_RANK = 512
MLA_KV_RANK = 256
MLA_NOPE = 128
MLA_ROPE = 64
MLA_V = 128
MLA_QK_PAD = 256
ROPE_THETA = 10000.0
SWA_Q_HEADS = 32
SWA_KV_HEADS = 4
SWA_GROUP = 8
SWA_DH = 64
SWA_BLOCK = 128
REL_BUCKETS = 32
REL_MAX_DIST = 128
D_FF = 5632
EPS = 1e-6
NEG = -1e30
LOG2E = 1.4426950408889634

COL_GATES = 0
COL_QS = 4096
COL_CQ = 6144
COL_KS = 6656
COL_VS = 6912
COL_CKV = 7168
COL_KRA = 7424
COL_KRB = 7552
IN_COLS_PAD = 7680

VMEM_LIMIT = 56 * 1024 * 1024


def _cparams(sem, flags=None):
    return pltpu.CompilerParams(dimension_semantics=sem, vmem_limit_bytes=VMEM_LIMIT, flags=flags)


def _rms(x, g):
    return x * lax.rsqrt(jnp.mean(x * x, axis=-1, keepdims=True) + EPS) * g


def _dot(a, b):
    return jnp.dot(a, b, preferred_element_type=F32)


def _dot_nt(a, b):
    return lax.dot_general(a, b, (((1,), (1,)), ((), ())), preferred_element_type=F32)


def _in_proj_body(x_ref, g_ref, w_ref, o_ref, hn_ref):
    @pl.when(pl.program_id(1) == 0)
    def _():
        hn_ref[...] = _rms(x_ref[...], g_ref[...]).astype(BF16)

    o_ref[...] = _dot(hn_ref[...], w_ref[...]).astype(o_ref.dtype)


def _in_proj(x2, gain, w, tm, tn):
    m = x2.shape[0]
    n = w.shape[1]
    return pl.pallas_call(
        _in_proj_body,
        grid=(m // tm, n // tn),
        in_specs=[
            pl.BlockSpec((tm, D_MODEL), lambda i, j: (i, 0)),
            pl.BlockSpec((1, D_MODEL), lambda i, j: (0, 0)),
            pl.BlockSpec((D_MODEL, tn), lambda i, j: (0, j)),
        ],
        out_specs=pl.BlockSpec((tm, tn), lambda i, j: (i, j)),
        out_shape=jax.ShapeDtypeStruct((m, n), BF16),
        scratch_shapes=[pltpu.VMEM((tm, D_MODEL), BF16)],
        compiler_params=_cparams(("parallel", "arbitrary")),
        name="in_proj",
    )(x2, gain, w)


def _qkv_up_body(cq_ref, ckv_ref, kra_ref, krb_ref, gq_ref, gkv_ref, wqt_ref, wk_ref, wvt_ref,
                 cq2_ref, sq2_ref, ck_ref, sk_ref, qt_ref, k_ref, vt_ref, *, qscale):
    ts = cq_ref.shape[0]
    cqn = _rms(cq_ref[...].astype(F32), gq_ref[...]).astype(BF16)
    ckvn = _rms(ckv_ref[...].astype(F32), gkv_ref[...]).astype(BF16)
    cq2 = cq2_ref[...]
    sq2 = sq2_ref[...]
    zeros_q = jnp.zeros((MLA_QK_PAD - MLA_NOPE - MLA_ROPE, ts), BF16)
    ones_v = jnp.ones((MLA_ACC_ROWS - MLA_V, ts), BF16)
    for h in range(MLA_HEADS):
        qf = _dot_nt(wqt_ref[h], cqn) * qscale
        rope = qf[MLA_NOPE:MLA_NOPE + MLA_ROPE] * cq2 + qf[MLA_NOPE + MLA_ROPE:] * sq2
        qt_ref[0, h, 0:MLA_NOPE, :] = qf[0:MLA_NOPE].astype(BF16)
        qt_ref[0, h, MLA_NOPE:MLA_NOPE + MLA_ROPE, :] = rope.astype(BF16)
        qt_ref[0, h, MLA_NOPE + MLA_ROPE:, :] = zeros_q
    kr = (kra_ref[...].astype(F32) * ck_ref[...] + krb_ref[...].astype(F32) * sk_ref[...]).astype(BF16)
    kall = _dot(ckvn, wk_ref[...]).astype(BF16)
    for h in range(MLA_HEADS):
        k_ref[0, h, :, 0:MLA_NOPE] = kall[:, h * MLA_NOPE:(h + 1) * MLA_NOPE]
        k_ref[0, h, :, MLA_NOPE:] = kr
        vt_ref[0, h, 0:MLA_V, :] = _dot_nt(wvt_ref[h], ckvn).astype(BF16)
        vt_ref[0, h, MLA_V:, :] = ones_v


def _qkv_up(p, gq, gkv, wqt, wk, wvt, cq2, sq2, ck, sk, b, s, ts):
    ns = s // ts
    row = lambda bi, i: bi * ns + i
    qscale = float((MLA_NOPE + MLA_ROPE) ** -0.5 * LOG2E)
    full = lambda shape: pl.BlockSpec(shape, lambda bi, i: (0,) * len(shape))
    return pl.pallas_call(
        functools.partial(_qkv_up_body, qscale=qscale),
        grid=(b, ns),
        in_specs=[
            pl.BlockSpec((ts, MLA_Q_RANK), lambda bi, i: (row(bi, i), COL_CQ // MLA_Q_RANK)),
            pl.BlockSpec((ts, MLA_KV_RANK), lambda bi, i: (row(bi, i), COL_CKV // MLA_KV_RANK)),
            pl.BlockSpec((ts, 128), lambda bi, i: (row(bi, i), COL_KRA // 128)),
            pl.BlockSpec((ts, 128), lambda bi, i: (row(bi, i), COL_KRB // 128)),
            full((1, MLA_Q_RANK)),
            full((1, MLA_KV_RANK)),
            full((MLA_HEADS, MLA_QK_PAD, MLA_Q_RANK)),
            full((MLA_KV_RANK, MLA_HEADS * MLA_NOPE)),
            full((MLA_HEADS, MLA_V, MLA_KV_RANK)),
            pl.BlockSpec((MLA_ROPE, ts), lambda bi, i: (0, i)),
            pl.BlockSpec((MLA_ROPE, ts), lambda bi, i: (0, i)),
            pl.BlockSpec((ts, 128), lambda bi, i: (i, 0)),
            pl.BlockSpec((ts, 128), lambda bi, i: (i, 0)),
        ],
        out_specs=[
            pl.BlockSpec((1, MLA_HEADS, MLA_QK_PAD, ts), lambda bi, i: (bi, 0, 0, i)),
            pl.BlockSpec((1, MLA_HEADS, ts, MLA_QK_PAD), lambda bi, i: (bi, 0, i, 0)),
            pl.BlockSpec((1, MLA_HEADS, MLA_ACC_ROWS, ts), lambda bi, i: (bi, 0, 0, i)),
        ],
        out_shape=[
            jax.ShapeDtypeStruct((b, MLA_HEADS, MLA_QK_PAD, s), BF16),
            jax.ShapeDtypeStruct((b, MLA_HEADS, s, MLA_QK_PAD), BF16),
            jax.ShapeDtypeStruct((b, MLA_HEADS, MLA_ACC_ROWS, s), BF16),
        ],
        compiler_params=_cparams(("parallel", "parallel")),
        name="qkv_up",
    )(p, p, p, p, gq, gkv, wqt, wk, wvt, cq2, sq2, ck, sk)


MLA_GW = 256
MLA_RB = 64
MLA_ACC_ROWS = 144


def _mla_body(qt_ref, k_ref, vt_ref, o_ref, acc_ref, m_ref, st0, st1, p0, p1, mc0, mc1, al0, al1, *, t):
    i = pl.program_id(2)
    st_s, p_s, mc_s, al_s = (st0, st1), (p0, p1), (mc0, mc1), (al0, al1)
    n_groups = t // MLA_GW

    acc_ref[...] = jnp.zeros_like(acc_ref)
    m_ref[...] = jnp.full_like(m_ref, NEG)

    def gcols(g):
        return slice(g * MLA_GW, (g + 1) * MLA_GW)

    def key_off(j):
        return pl.multiple_of(jnp.where(j == 0, i, j - 1) * t, t)

    def qk_diag(g):
        cols = gcols(g)
        rows = (g + 1) * MLA_GW
        k0 = pl.multiple_of(i * t, t)
        st = _dot(k_ref[0, 0, pl.ds(k0, rows), :], qt_ref[0, 0, :, cols])
        kpos = lax.broadcasted_iota(jnp.int32, (MLA_GW, MLA_GW), 0)
        qpos = lax.broadcasted_iota(jnp.int32, (MLA_GW, MLA_GW), 1)
        tri = jnp.where(kpos <= qpos, st[rows - MLA_GW:], NEG)
        mc = jnp.max(tri, axis=0, keepdims=True)
        st0[rows - MLA_GW:rows, cols] = tri
        if rows > MLA_GW:
            st0[0:rows - MLA_GW, cols] = st[0:rows - MLA_GW]
            mc = jnp.maximum(mc, jnp.max(st[0:rows - MLA_GW], axis=0, keepdims=True))
        if rows < t:
            st0[rows:, cols] = jnp.full((t - rows, MLA_GW), NEG, F32)
        mc0[:, cols] = mc

    def qk(j, slot, g):
        cols = gcols(g)
        k0 = pl.multiple_of((j - 1) * t, t)
        st = _dot(k_ref[0, 0, pl.ds(k0, t), :], qt_ref[0, 0, :, cols])
        st_s[slot][:, cols] = st
        mc_s[slot][:, cols] = jnp.max(st, axis=0, keepdims=True)

    def softmax(slot, g):
        cols = gcols(g)
        m_old = m_ref[:, cols]
        m_new = jnp.maximum(m_old, mc_s[slot][:, cols])
        al_s[slot][:, cols] = jnp.exp2(m_old - m_new)
        m_ref[:, cols] = m_new
        for r in range(0, t, MLA_RB):
            p = jnp.exp2(st_s[slot][r:r + MLA_RB, cols] - m_new)
            p_s[slot][r:r + MLA_RB, cols] = p.astype(BF16)

    def pv(j, slot, g):
        cols = gcols(g)
        vt1 = vt_ref[0, 0, :, pl.ds(key_off(j), t)]
        acc_ref[:, cols] = al_s[slot][:, cols] * acc_ref[:, cols] + _dot(vt1, p_s[slot][:, cols])

    def step(k, slot):
        for g in range(n_groups):
            qk(k, slot, g)
            softmax(1 - slot, g)
            pv(k - 1, 1 - slot, g)

    def drain(j, slot):
        for g in range(n_groups):
            softmax(slot, g)
            pv(j, slot, g)

    for g in range(n_groups):
        qk_diag(g)

    def pair(u, carry):
        step(2 * u + 1, 1)
        step(2 * u + 2, 0)
        return carry

    lax.fori_loop(0, i // 2, pair, 0)

    @pl.when(i % 2 == 1)
    def _():
        step(i, 1)
        drain(i, 1)

    @pl.when(i % 2 == 0)
    def _():
        drain(i, 0)

    out_t = acc_ref[0:MLA_V, :] / acc_ref[MLA_V:MLA_V + 1, :]
    o_ref[0] = out_t.T.astype(o_ref.dtype)


def _mla_attention(qt, k, vt, t):
    b, h, _, s = qt.shape
    return pl.pallas_call(
        functools.partial(_mla_body, t=t),
        grid=(b, h, s // t),
        in_specs=[
            pl.BlockSpec((1, 1, MLA_QK_PAD, t), lambda bi, hi, i: (bi, hi, 0, i)),
            pl.BlockSpec((1, 1, s, MLA_QK_PAD), lambda bi, hi, i: (bi, hi, 0, 0)),
            pl.BlockSpec((1, 1, MLA_ACC_ROWS, s), lambda bi, hi, i: (bi, hi, 0, 0)),
        ],
        out_specs=pl.BlockSpec((1, t, MLA_V), lambda bi, hi, i: (bi, i, hi)),
        out_shape=jax.ShapeDtypeStruct((b, s, h * MLA_V), BF16),
        scratch_shapes=[
            pltpu.VMEM((MLA_ACC_ROWS, t), F32),
            pltpu.VMEM((1, t), F32),
            pltpu.VMEM((t, t), F32), pltpu.VMEM((t, t), F32),
            pltpu.VMEM((t, t), BF16), pltpu.VMEM((t, t), BF16),
            pltpu.VMEM((1, t), F32), pltpu.VMEM((1, t), F32),
            pltpu.VMEM((1, t), F32), pltpu.VMEM((1, t), F32),
        ],
        compiler_params=_cparams(("parallel", "parallel", "arbitrary")),
        name="mla_attn",
    )(qt, k, vt)


def _t5_bucket_table():
    n = np.arange(REL_MAX_DIST)
    max_exact = REL_BUCKETS // 2
    large = max_exact + (np.log(np.maximum(n, 1).astype(np.float32) / max_exact)
                         / math.log(REL_MAX_DIST / max_exact) * (REL_BUCKETS - max_exact)).astype(np.int32)
    large = np.minimum(large, REL_BUCKETS - 1)
    return np.where(n < max_exact, n, large)


def _swa_bias_body(bucket_ref, table_ref, o_ref):
    bucket = bucket_ref[...]
    kj = lax.broadcasted_iota(jnp.int32, bucket.shape, 1)
    for h in range(SWA_Q_HEADS):
        bias = jnp.zeros(bucket.shape, F32)
        for bk in range(REL_BUCKETS):
            bias = jnp.where(bucket == bk, table_ref[bk, h], bias)
        valid = bucket >= 0
        o_ref[1, h] = jnp.where(valid, bias, NEG)
        o_ref[0, h] = jnp.where(valid & (kj >= SWA_BLOCK), bias, NEG)


def _swa_bias(rel_table):
    qi = np.arange(SWA_BLOCK)[:, None]
    kj = np.arange(2 * SWA_BLOCK)[None, :]
    dist = qi + SWA_BLOCK - kj
    in_window = (dist >= 0) & (dist < SWA_BLOCK)
    bucket = np.where(in_window, _t5_bucket_table()[np.clip(dist, 0, REL_MAX_DIST - 1)], -1).astype(np.int32)
    return pl.pallas_call(
        _swa_bias_body,
        in_specs=[
            pl.BlockSpec(memory_space=pltpu.VMEM),
            pl.BlockSpec(memory_space=pltpu.SMEM),
        ],
        out_specs=pl.BlockSpec(memory_space=pltpu.VMEM),
        out_shape=jax.ShapeDtypeStruct((2, SWA_Q_HEADS, SWA_BLOCK, 2 * SWA_BLOCK), F32),
        name="swa_bias",
    )(jnp.asarray(bucket), rel_table)


def _swa_body(q_ref, kc_ref, kp_ref, vc_ref, vp_ref, bias_ref, sink_ref, o_ref):
    blk = SWA_BLOCK
    lane = lax.broadcasted_iota(jnp.int32, (2 * blk, 128), 1)
    lane_q = lax.broadcasted_iota(jnp.int32, (blk, 128), 1)
    kband = jnp.concatenate([kp_ref[...], kc_ref[...]], axis=0)
    vband = jnp.concatenate([vp_ref[...], vc_ref[...]], axis=0)

    def dup_half(x2, half):
        rolled = pltpu.roll(x2, 64, axis=1)
        first = lane < 64
        return jnp.where(first, x2, rolled) if half == 0 else jnp.where(first, rolled, x2)

    for kvh in range(SWA_KV_HEADS):
        pair_cols = slice((kvh // 2) * 128, (kvh // 2 + 1) * 128)
        k2 = dup_half(kband[:, pair_cols].astype(F32), kvh % 2).astype(BF16)
        v2 = dup_half(vband[:, pair_cols].astype(F32), kvh % 2).astype(BF16)
        parts = []
        for j in range(SWA_GROUP // 2):
            c0 = kvh * SWA_GROUP * SWA_DH + j * 128
            q2 = q_ref[:, c0:c0 + 128]
            zero = jnp.zeros_like(q2)
            parts.append(jnp.where(lane_q < 64, q2, zero))
            parts.append(jnp.where(lane_q < 64, zero, q2))
        qstack = jnp.concatenate(parts, axis=0)
        s_all = _dot_nt(qstack, k2) * (SWA_DH ** -0.5)
        ps = []
        dens = []
        for g in range(SWA_GROUP):
            hq = kvh * SWA_GROUP + g
            sg = s_all[g * blk:(g + 1) * blk] + bias_ref[0, hq]
            sink = sink_ref[hq]
            m = jnp.maximum(jnp.max(sg, axis=-1, keepdims=True), sink)
            p = jnp.exp(sg - m)
            dens.append(jnp.sum(p, axis=-1, keepdims=True) + jnp.exp(sink - m))
            ps.append(p.astype(BF16))
        o_all = _dot(jnp.concatenate(ps, axis=0), v2)
        for j in range(SWA_GROUP // 2):
            oa = o_all[(2 * j) * blk:(2 * j + 1) * blk] / dens[2 * j]
            ob = o_all[(2 * j + 1) * blk:(2 * j + 2) * blk] / dens[2 * j + 1]
            c0 = kvh * SWA_GROUP * SWA_DH + j * 128
            o_ref[:, c0:c0 + 128] = jnp.where(lane_q < 64, oa, ob).astype(o_ref.dtype)


def _swa_attention(p, bias, sinks, b, s):
    nb = s // SWA_BLOCK
    kvw = SWA_KV_HEADS * SWA_DH
    row = lambda bi, n: bi * nb + n
    prow = lambda bi, n: bi * nb + jnp.maximum(n - 1, 0)
    return pl.pallas_call(
        _swa_body,
        grid=(b, nb),
        in_specs=[
            pl.BlockSpec((SWA_BLOCK, D_MODEL), lambda bi, n: (row(bi, n), COL_QS // D_MODEL)),
            pl.BlockSpec((SWA_BLOCK, kvw), lambda bi, n: (row(bi, n), COL_KS // kvw)),
            pl.BlockSpec((SWA_BLOCK, kvw), lambda bi, n: (prow(bi, n), COL_KS // kvw)),
            pl.BlockSpec((SWA_BLOCK, kvw), lambda bi, n: (row(bi, n), COL_VS // kvw)),
            pl.BlockSpec((SWA_BLOCK, kvw), lambda bi, n: (prow(bi, n), COL_VS // kvw)),
            pl.BlockSpec((1, SWA_Q_HEADS, SWA_BLOCK, 2 * SWA_BLOCK),
                         lambda bi, n: (jnp.minimum(n, 1), 0, 0, 0)),
            pl.BlockSpec(memory_space=pltpu.SMEM),
        ],
        out_specs=pl.BlockSpec((SWA_BLOCK, D_MODEL), lambda bi, n: (row(bi, n), 0)),
        out_shape=jax.ShapeDtypeStruct((b * s, D_MODEL), BF16),
        compiler_params=_cparams(("parallel", "arbitrary")),
        name="swa_attn",
    )(p, p, p, p, p, bias, sinks)


def _merge_body(oa_ref, ob_ref, wa_ref, wb_ref, g0_ref, g1_ref, o_ref):
    ya = _dot(oa_ref[...], wa_ref[...])
    yb = _dot(ob_ref[...], wb_ref[...])
    g0 = jax.nn.sigmoid(g0_ref[...].astype(F32))
    g1 = jax.nn.sigmoid(g1_ref[...].astype(F32))
    o_ref[...] = (g0 * ya + g1 * yb).astype(o_ref.dtype)


def _merge(oa, ob, wa, wb, p, tm, tn):
    m = oa.shape[0]
    nj = D_MODEL // tn
    return pl.pallas_call(
        _merge_body,
        grid=(m // tm, nj),
        in_specs=[
            pl.BlockSpec((tm, D_MODEL), lambda i, j: (i, 0)),
            pl.BlockSpec((tm, D_MODEL), lambda i, j: (i, 0)),
            pl.BlockSpec((D_MODEL, tn), lambda i, j: (0, j)),
            pl.BlockSpec((D_MODEL, tn), lambda i, j: (0, j)),
            pl.BlockSpec((tm, tn), lambda i, j: (i, COL_GATES // tn + j)),
            pl.BlockSpec((tm, tn), lambda i, j: (i, COL_GATES // tn + nj + j)),
        ],
        out_specs=pl.BlockSpec((tm, tn), lambda i, j: (i, j)),
        out_shape=jax.ShapeDtypeStruct((m, D_MODEL), BF16),
        compiler_params=_cparams(("parallel", "arbitrary")),
        name="gated_merge",
    )(oa, ob, wa, wb, p, p)


def _out_proj_body(mg_ref, w_ref, x_ref, g_ref, o_ref):
    y = _dot(mg_ref[...], w_ref[...])
    o_ref[...] = x_ref[...] + _rms(y, g_ref[...])


def _out_proj(mg, w, x2, gain, tm):
    m = mg.shape[0]
    return pl.pallas_call(
        _out_proj_body,
        grid=(m // tm,),
        in_specs=[
            pl.BlockSpec((tm, D_MODEL), lambda i: (i, 0)),
            pl.BlockSpec((D_MODEL, D_MODEL), lambda i: (0, 0)),
            pl.BlockSpec((tm, D_MODEL), lambda i: (i, 0)),
            pl.BlockSpec((1, D_MODEL), lambda i: (0, 0)),
        ],
        out_specs=pl.BlockSpec((tm, D_MODEL), lambda i: (i, 0)),
        out_shape=jax.ShapeDtypeStruct((m, D_MODEL), F32),
        compiler_params=_cparams(("parallel",)),
        name="out_proj",
    )(mg, w, x2, gain)


HALO = 16


def _ffn_up_body(x_ref, xh_ref, g_ref, wa_ref, wb_ref, cw_ref, cb_ref, o_ref, hn_ref, a_ref, *, tiles_per_seq):
    i = pl.program_id(0)
    tm = x_ref.shape[0]

    @pl.when(pl.program_id(1) == 0)
    def _():
        halo = _rms(xh_ref[...], g_ref[...])
        halo = jnp.where(i % tiles_per_seq == 0, jnp.zeros_like(halo), halo)
        hn_ref[0:HALO, :] = halo.astype(BF16)
        hn_ref[HALO:, :] = _rms(x_ref[...], g_ref[...]).astype(BF16)

    a_ref[...] = _dot(hn_ref[...], wa_ref[...])
    gate = _dot(hn_ref[HALO:, :], wb_ref[...])
    c = cb_ref[...] + cw_ref[0:1, :] * a_ref[pl.ds(HALO - 2, tm), :]
    c = c + cw_ref[1:2, :] * a_ref[pl.ds(HALO - 1, tm), :]
    c = c + cw_ref[2:3, :] * a_ref[pl.ds(HALO, tm), :]
    o_ref[...] = (jax.nn.gelu(c, approximate=True) * gate).astype(o_ref.dtype)


def _ffn_up(x1, gain, w_up, conv_w, conv_b, s, tm, tn):
    m = x1.shape[0]
    nj = D_FF // tn
    hb = tm // HALO
    return pl.pallas_call(
        functools.partial(_ffn_up_body, tiles_per_seq=s // tm),
        grid=(m // tm, nj),
        in_specs=[
            pl.BlockSpec((tm, D_MODEL), lambda i, j: (i, 0)),
            pl.BlockSpec((HALO, D_MODEL), lambda i, j: (jnp.maximum(i * hb - 1, 0), 0)),
            pl.BlockSpec((1, D_MODEL), lambda i, j: (0, 0)),
            pl.BlockSpec((D_MODEL, tn), lambda i, j: (0, j)),
            pl.BlockSpec((D_MODEL, tn), lambda i, j: (0, nj + j)),
            pl.BlockSpec((3, tn), lambda i, j: (0, j)),
            pl.BlockSpec((1, tn), lambda i, j: (0, j)),
        ],
        out_specs=pl.BlockSpec((tm, tn), lambda i, j: (i, j)),
        out_shape=jax.ShapeDtypeStruct((m, D_FF), BF16),
        scratch_shapes=[pltpu.VMEM((HALO + tm, D_MODEL), BF16), pltpu.VMEM((HALO + tm, tn), F32)],
        compiler_params=_cparams(("parallel", "arbitrary")),
        name="ffn_up_conv",
    )(x1, x1, gain, w_up, w_up, conv_w, conv_b)


def _ffn_down_body(t_ref, w_ref, x_ref, g_ref, o_ref, acc_ref):
    kk = pl.program_id(1)

    @pl.when(kk == 0)
    def _():
        acc_ref[...] = jnp.zeros_like(acc_ref)

    acc_ref[...] += _dot(t_ref[...], w_ref[...])

    @pl.when(kk == pl.num_programs(1) - 1)
    def _():
        o_ref[...] = x_ref[...] + _rms(acc_ref[...], g_ref[...])


def _ffn_down(t, w, x1, gain, tm, tk):
    m = t.shape[0]
    return pl.pallas_call(
        _ffn_down_body,
        grid=(m // tm, D_FF // tk),
        in_specs=[
            pl.BlockSpec((tm, tk), lambda i, k: (i, k)),
            pl.BlockSpec((tk, D_MODEL), lambda i, k: (k, 0)),
            pl.BlockSpec((tm, D_MODEL), lambda i, k: (i, 0)),
            pl.BlockSpec((1, D_MODEL), lambda i, k: (0, 0)),
        ],
        out_specs=pl.BlockSpec((tm, D_MODEL), lambda i, k: (i, 0)),
        out_shape=jax.ShapeDtypeStruct((m, D_MODEL), F32),
        scratch_shapes=[pltpu.VMEM((tm, D_MODEL), F32)],
        compiler_params=_cparams(("parallel", "arbitrary")),
        name="ffn_down",
    )(t, w, x1, gain)


def _prep_w_in(w_in):
    cq = w_in[:, 0:512]
    ckv = w_in[:, 512:768]
    kr = w_in[:, 768:832]
    qs = w_in[:, 832:2880]
    ks = w_in[:, 2880:3136]
    vs = w_in[:, 3136:3392]
    gates = w_in[:, 3392:7488]
    z64 = jnp.zeros((D_MODEL, 64), w_in.dtype)
    kr_sw = jnp.concatenate([kr[:, 32:], kr[:, :32]], axis=1)
    w = jnp.concatenate([gates, qs, cq, ks, vs, ckv, kr, z64, kr_sw, z64], axis=1)
    return w.astype(BF16)


def _prep_w_q_up(w):
    w = w.reshape(MLA_Q_RANK, MLA_HEADS, MLA_NOPE + MLA_ROPE)
    nope = w[:, :, :MLA_NOPE]
    rope = w[:, :, MLA_NOPE:]
    rope_sw = jnp.concatenate([rope[:, :, 32:], rope[:, :, :32]], axis=2)
    wq = jnp.concatenate([nope, rope, rope_sw], axis=2)
    return jnp.transpose(wq, (1, 2, 0)).astype(BF16)


def _prep_w_kv_up(w):
    w = w.reshape(MLA_KV_RANK, MLA_HEADS, MLA_NOPE + MLA_V)
    wk = w[:, :, :MLA_NOPE].reshape(MLA_KV_RANK, MLA_HEADS * MLA_NOPE).astype(BF16)
    wvt = jnp.transpose(w[:, :, MLA_NOPE:], (1, 2, 0)).astype(BF16)
    return wk, wvt


def _rope_tables(s):
    inv = ROPE_THETA ** (-jnp.arange(0, MLA_ROPE, 2, dtype=F32) / MLA_ROPE)
    ang = jnp.arange(s).astype(F32)[:, None] * inv[None, :]
    cos, sin = jnp.cos(ang), jnp.sin(ang)
    z = jnp.zeros((s, 64), F32)
    ck = jnp.concatenate([cos, cos, z], axis=1)
    sk = jnp.concatenate([-sin, sin, z], axis=1)
    cq2 = jnp.concatenate([cos, cos], axis=1).T
    sq2 = jnp.concatenate([-sin, sin], axis=1).T
    return cq2, sq2, ck, sk


def _tile(n, pref):
    t = min(n, pref)
    assert n % t == 0, (n, pref)
    return t


def kernel(x, norm_mix_pre, norm_mix_post, norm_ffn_pre, norm_ffn_post, w_in, mla_q_norm, mla_w_q_up, mla_kv_norm, mla_w_kv_up, swa_sinks, rel_bias_table, w_o_mla, w_o_swa, w_out, ffn_w_up, ffn_conv_w, ffn_conv_b, ffn_w_down):
    b, s, d = x.shape
    assert d == D_MODEL and s % SWA_BLOCK == 0
    depth = w_in.shape[0]
    m = b * s
    x2 = x.reshape(m, d)
    cq2, sq2, ck, sk = _rope_tables(s)
    bias = _swa_bias(rel_bias_table.astype(F32))
    for l in range(depth):
        p = _in_proj(x2, norm_mix_pre[l][None], _prep_w_in(w_in[l]), _tile(m, 1024), 1280)
        wk, wvt = _prep_w_kv_up(mla_w_kv_up[l])
        qt, k, vt = _qkv_up(p, mla_q_norm[l][None], mla_kv_norm[l][None], _prep_w_q_up(mla_w_q_up[l]),
                            wk, wvt, cq2, sq2, ck, sk, b, s, _tile(s, 512))
        o_a = _mla_attention(qt, k, vt, _tile(s, 1024)).reshape(m, d)
        o_b = _swa_attention(p, bias, swa_sinks[l].astype(F32), b, s)
        mg = _merge(o_a, o_b, w_o_mla[l].astype(BF16), w_o_swa[l].astype(BF16), p, _tile(m, 512), 1024)
        x1 = _out_proj(mg, w_out[l].astype(BF16), x2, norm_mix_post[l][None], _tile(m, 512))
        t = _ffn_up(x1, norm_ffn_pre[l][None], ffn_w_up[l].astype(BF16), ffn_conv_w[l], ffn_conv_b[l][None],
                    s, _tile(s, 512), 1408)
        x2 = _ffn_down(t, ffn_w_down[l].astype(BF16), x1, norm_ffn_post[l][None], _tile(m, 512), 1408)
    return x2.reshape(b, s, d)
```

```python
import functools
import math

import numpy as np
import jax
import jax.numpy as jnp
from jax import lax
from jax.experimental import pallas as pl
from jax.experimental.pallas import tpu as pltpu

F32 = jnp.float32
BF16 = jnp.bfloat16

D_MODEL = 2048
MLA_HEADS = 16
MLA_Q_RANK = 512
MLA_KV_RANK = 256
MLA_NOPE = 128
MLA_ROPE = 64
MLA_V = 128
MLA_QK_PAD = 256
ROPE_THETA = 10000.0
SWA_Q_HEADS = 32
SWA_KV_HEADS = 4
SWA_GROUP = 8
SWA_DH = 64
SWA_BLOCK = 128
REL_BUCKETS = 32
REL_MAX_DIST = 128
D_FF = 5632
EPS = 1e-6
NEG = -1e30
LOG2E = 1.4426950408889634
SWA_QSCALE = SWA_DH ** -0.5 * LOG2E

COL_GATES = 0
COL_QS = 4096
COL_CQ = 6144
COL_KS = 6656
COL_VS = 6912
COL_CKV = 7168
COL_KRA = 7424
COL_KRB = 7552
IN_COLS_PAD = 7680

VMEM_LIMIT = 56 * 1024 * 1024


def _cparams(sem, flags=None):
    return pltpu.CompilerParams(dimension_semantics=sem, vmem_limit_bytes=VMEM_LIMIT, flags=flags)


def _rms(x, g):
    return x * lax.rsqrt(jnp.mean(x * x, axis=-1, keepdims=True) + EPS) * g


def _dot(a, b):
    return jnp.dot(a, b, preferred_element_type=F32)


def _dot_nt(a, b):
    return lax.dot_general(a, b, (((1,), (1,)), ((), ())), preferred_element_type=F32)


def _in_proj_body(x_ref, g_ref, w_ref, o_ref, hn_ref):
    @pl.when(pl.program_id(1) == 0)
    def _():
        hn_ref[...] = _rms(x_ref[...], g_ref[...]).astype(BF16)

    o_ref[...] = _dot(hn_ref[...], w_ref[...]).astype(o_ref.dtype)


def _in_proj(x2, gain, w, tm, tn):
    m = x2.shape[0]
    n = w.shape[1]
    return pl.pallas_call(
        _in_proj_body,
        grid=(m // tm, n // tn),
        in_specs=[
            pl.BlockSpec((tm, D_MODEL), lambda i, j: (i, 0)),
            pl.BlockSpec((1, D_MODEL), lambda i, j: (0, 0)),
            pl.BlockSpec((D_MODEL, tn), lambda i, j: (0, j)),
        ],
        out_specs=pl.BlockSpec((tm, tn), lambda i, j: (i, j)),
        out_shape=jax.ShapeDtypeStruct((m, n), BF16),
        scratch_shapes=[pltpu.VMEM((tm, D_MODEL), BF16)],
        compiler_params=_cparams(("parallel", "arbitrary")),
        name="in_proj",
    )(x2, gain, w)


def _qkv_up_body(cq_ref, ckv_ref, kra_ref, krb_ref, gq_ref, gkv_ref, wqt_ref, wk_ref, wvt_ref,
                 cq2_ref, sq2_ref, ck_ref, sk_ref, qt_ref, k_ref, vt_ref, *, qscale):
    ts = cq_ref.shape[0]
    cqn = _rms(cq_ref[...].astype(F32), gq_ref[...]).astype(BF16)
    ckvn = _rms(ckv_ref[...].astype(F32), gkv_ref[...]).astype(BF16)
    cq2 = cq2_ref[...]
    sq2 = sq2_ref[...]
    zeros_q = jnp.zeros((MLA_QK_PAD - MLA_NOPE - MLA_ROPE, ts), BF16)
    ones_v = jnp.ones((MLA_ACC_ROWS - MLA_V, ts), BF16)
    for h in range(MLA_HEADS):
        qf = _dot_nt(wqt_ref[h], cqn) * qscale
        rope = qf[MLA_NOPE:MLA_NOPE + MLA_ROPE] * cq2 + qf[MLA_NOPE + MLA_ROPE:] * sq2
        qt_ref[0, h, 0:MLA_NOPE, :] = qf[0:MLA_NOPE].astype(BF16)
        qt_ref[0, h, MLA_NOPE:MLA_NOPE + MLA_ROPE, :] = rope.astype(BF16)
        qt_ref[0, h, MLA_NOPE + MLA_ROPE:, :] = zeros_q
    kr = (kra_ref[...].astype(F32) * ck_ref[...] + krb_ref[...].astype(F32) * sk_ref[...]).astype(BF16)
    kall = _dot(ckvn, wk_ref[...]).astype(BF16)
    for h in range(MLA_HEADS):
        k_ref[0, h, :, 0:MLA_NOPE] = kall[:, h * MLA_NOPE:(h + 1) * MLA_NOPE]
        k_ref[0, h, :, MLA_NOPE:] = kr
        vt_ref[0, h, 0:MLA_V, :] = _dot_nt(wvt_ref[h], ckvn).astype(BF16)
        vt_ref[0, h, MLA_V:, :] = ones_v


def _qkv_up(p, gq, gkv, wqt, wk, wvt, cq2, sq2, ck, sk, b, s, ts):
    ns = s // ts
    row = lambda bi, i: bi * ns + i
    qscale = float((MLA_NOPE + MLA_ROPE) ** -0.5 * LOG2E)
    full = lambda shape: pl.BlockSpec(shape, lambda bi, i: (0,) * len(shape))
    return pl.pallas_call(
        functools.partial(_qkv_up_body, qscale=qscale),
        grid=(b, ns),
        in_specs=[
            pl.BlockSpec((ts, MLA_Q_RANK), lambda bi, i: (row(bi, i), COL_CQ // MLA_Q_RANK)),
            pl.BlockSpec((ts, MLA_KV_RANK), lambda bi, i: (row(bi, i), COL_CKV // MLA_KV_RANK)),
            pl.BlockSpec((ts, 128), lambda bi, i: (row(bi, i), COL_KRA // 128)),
            pl.BlockSpec((ts, 128), lambda bi, i: (row(bi, i), COL_KRB // 128)),
            full((1, MLA_Q_RANK)),
            full((1, MLA_KV_RANK)),
            full((MLA_HEADS, MLA_QK_PAD, MLA_Q_RANK)),
            full((MLA_KV_RANK, MLA_HEADS * MLA_NOPE)),
            full((MLA_HEADS, MLA_V, MLA_KV_RANK)),
            pl.BlockSpec((MLA_ROPE, ts), lambda bi, i: (0, i)),
            pl.BlockSpec((MLA_ROPE, ts), lambda bi, i: (0, i)),
            pl.BlockSpec((ts, 128), lambda bi, i: (i, 0)),
            pl.BlockSpec((ts, 128), lambda bi, i: (i, 0)),
        ],
        out_specs=[
            pl.BlockSpec((1, MLA_HEADS, MLA_QK_PAD, ts), lambda bi, i: (bi, 0, 0, i)),
            pl.BlockSpec((1, MLA_HEADS, ts, MLA_QK_PAD), lambda bi, i: (bi, 0, i, 0)),
            pl.BlockSpec((1, MLA_HEADS, MLA_ACC_ROWS, ts), lambda bi, i: (bi, 0, 0, i)),
        ],
        out_shape=[
            jax.ShapeDtypeStruct((b, MLA_HEADS, MLA_QK_PAD, s), BF16),
            jax.ShapeDtypeStruct((b, MLA_HEADS, s, MLA_QK_PAD), BF16),
            jax.ShapeDtypeStruct((b, MLA_HEADS, MLA_ACC_ROWS, s), BF16),
        ],
        compiler_params=_cparams(("parallel", "parallel")),
        name="qkv_up",
    )(p, p, p, p, gq, gkv, wqt, wk, wvt, cq2, sq2, ck, sk)


MLA_GW = 256
MLA_RB = 64
MLA_QK_ROWS = 512
MLA_ACC_ROWS = 144


def _mla_body(qt_ref, k_ref, vt_ref, o_ref, acc_ref, m_ref, st0, st1, p0, p1, mc0, mc1, al0, al1, *, t):
    i = pl.program_id(2)
    st_s, p_s, mc_s, al_s = (st0, st1), (p0, p1), (mc0, mc1), (al0, al1)
    n_groups = t // MLA_GW

    acc_ref[...] = jnp.zeros_like(acc_ref)
    m_ref[...] = jnp.full_like(m_ref, NEG)

    def qt_group(g):
        return qt_ref[0, 0, :, g * MLA_GW:(g + 1) * MLA_GW]

    def key_off(j):
        return pl.multiple_of(jnp.where(j == 0, i, j - 1) * t, t)

    def qk_diag(g):
        rows = (g + 1) * MLA_GW
        k0 = pl.multiple_of(i * t, t)
        st = _dot(k_ref[0, 0, pl.ds(k0, rows), :], qt_group(g))
        kpos = lax.broadcasted_iota(jnp.int32, (MLA_GW, MLA_GW), 0)
        qpos = lax.broadcasted_iota(jnp.int32, (MLA_GW, MLA_GW), 1)
        tri = jnp.where(kpos <= qpos, st[rows - MLA_GW:], NEG)
        mc = jnp.max(tri, axis=0, keepdims=True)
        st0[g, rows - MLA_GW:rows, :] = tri
        if rows > MLA_GW:
            st0[g, 0:rows - MLA_GW, :] = st[0:rows - MLA_GW]
            mc = jnp.maximum(mc, jnp.max(st[0:rows - MLA_GW], axis=0, keepdims=True))
        if rows < t:
            st0[g, rows:, :] = jnp.full((t - rows, MLA_GW), NEG, F32)
        mc0[g] = mc

    def qk(j, slot, g):
        mc = None
        for r in range(0, t, MLA_QK_ROWS):
            k0 = pl.multiple_of((j - 1) * t + r, MLA_QK_ROWS)
            st = _dot(k_ref[0, 0, pl.ds(k0, MLA_QK_ROWS), :], qt_group(g))
            st_s[slot][g, r:r + MLA_QK_ROWS, :] = st
            mx = jnp.max(st, axis=0, keepdims=True)
            mc = mx if mc is None else jnp.maximum(mc, mx)
        mc_s[slot][g] = mc

    def softmax(slot, g):
        m_old = m_ref[g]
        m_new = jnp.maximum(m_old, mc_s[slot][g])
        al_s[slot][g] = jnp.exp2(m_old - m_new)
        m_ref[g] = m_new
        for r in range(0, t, MLA_RB):
            p = jnp.exp2(st_s[slot][g, r:r + MLA_RB, :] - m_new)
            p_s[slot][g, r:r + MLA_RB, :] = p.astype(BF16)

    def pv(j, slot, g):
        vt1 = vt_ref[0, 0, :, pl.ds(key_off(j), t)]
        acc_ref[g] = al_s[slot][g] * acc_ref[g] + _dot(vt1, p_s[slot][g])

    def step(k, slot):
        for g in range(n_groups):
            qk(k, slot, g)
            softmax(1 - slot, g)
            pv(k - 1, 1 - slot, g)

    def drain(j, slot):
        for g in range(n_groups):
            softmax(slot, g)
            pv(j, slot, g)

    for g in range(n_groups):
        qk_diag(g)

    def pair(u, carry):
        step(2 * u + 1, 1)
        step(2 * u + 2, 0)
        return carry

    lax.fori_loop(0, i // 2, pair, 0)

    @pl.when(i % 2 == 1)
    def _():
        step(i, 1)
        drain(i, 1)

    @pl.when(i % 2 == 0)
    def _():
        drain(i, 0)

    for g in range(n_groups):
        out_t = acc_ref[g, 0:MLA_V, :] / acc_ref[g, MLA_V:MLA_V + 1, :]
        o_ref[0, g * MLA_GW:(g + 1) * MLA_GW, :] = out_t.T.astype(o_ref.dtype)


def _mla_attention(qt, k, vt, t):
    b, h, _, s = qt.shape
    n_groups = t // MLA_GW
    per_group = lambda rows, dtype: pltpu.VMEM((n_groups, rows, MLA_GW), dtype)
    return pl.pallas_call(
        functools.partial(_mla_body, t=t),
        grid=(b, h, s // t),
        in_specs=[
            pl.BlockSpec((1, 1, MLA_QK_PAD, t), lambda bi, hi, i: (bi, hi, 0, i)),
            pl.BlockSpec((1, 1, s, MLA_QK_PAD), lambda bi, hi, i: (bi, hi, 0, 0)),
            pl.BlockSpec((1, 1, MLA_ACC_ROWS, s), lambda bi, hi, i: (bi, hi, 0, 0)),
        ],
        out_specs=pl.BlockSpec((1, t, MLA_V), lambda bi, hi, i: (bi, i, hi)),
        out_shape=jax.ShapeDtypeStruct((b, s, h * MLA_V), BF16),
        scratch_shapes=[
            per_group(MLA_ACC_ROWS, F32),
            per_group(1, F32),
            per_group(t, F32), per_group(t, F32),
            per_group(t, BF16), per_group(t, BF16),
            per_group(1, F32), per_group(1, F32),
            per_group(1, F32), per_group(1, F32),
        ],
        compiler_params=_cparams(("parallel", "parallel", "arbitrary")),
        name="mla_attn",
    )(qt, k, vt)


def _t5_bucket_table():
    n = np.arange(REL_MAX_DIST)
    max_exact = REL_BUCKETS // 2
    large = max_exact + (np.log(np.maximum(n, 1).astype(np.float32) / max_exact)
                         / math.log(REL_MAX_DIST / max_exact) * (REL_BUCKETS - max_exact)).astype(np.int32)
    large = np.minimum(large, REL_BUCKETS - 1)
    return np.where(n < max_exact, n, large)


def _swa_bias_body(bucket_ref, table_ref, o_ref):
    bucket = bucket_ref[...]
    kj = lax.broadcasted_iota(jnp.int32, bucket.shape, 1)
    for h in range(SWA_Q_HEADS):
        bias = jnp.zeros(bucket.shape, F32)
        for bk in range(REL_BUCKETS):
            bias = jnp.where(bucket == bk, table_ref[bk, h], bias)
        valid = bucket >= 0
        bias = bias * LOG2E
        o_ref[1, h] = jnp.where(valid, bias, NEG)
        o_ref[0, h] = jnp.where(valid & (kj >= SWA_BLOCK), bias, NEG)


def _swa_bias(rel_table):
    qi = np.arange(SWA_BLOCK)[:, None]
    kj = np.arange(2 * SWA_BLOCK)[None, :]
    dist = qi + SWA_BLOCK - kj
    in_window = (dist >= 0) & (dist < SWA_BLOCK)
    bucket = np.where(in_window, _t5_bucket_table()[np.clip(dist, 0, REL_MAX_DIST - 1)], -1).astype(np.int32)
    return pl.pallas_call(
        _swa_bias_body,
        in_specs=[
            pl.BlockSpec(memory_space=pltpu.VMEM),
            pl.BlockSpec(memory_space=pltpu.SMEM),
        ],
        out_specs=pl.BlockSpec(memory_space=pltpu.VMEM),
        out_shape=jax.ShapeDtypeStruct((2, SWA_Q_HEADS, SWA_BLOCK, 2 * SWA_BLOCK), F32),
        name="swa_bias",
    )(jnp.asarray(bucket), rel_table)


def _swa_body(q_ref, kc_ref, kp_ref, vc_ref, vp_ref, bias_ref, sink_ref, o_ref):
    blk = SWA_BLOCK
    lane = lax.broadcasted_iota(jnp.int32, (2 * blk, 128), 1)
    lane_q = lax.broadcasted_iota(jnp.int32, (blk, 128), 1)
    kband = jnp.concatenate([kp_ref[...], kc_ref[...]], axis=0)
    vband = jnp.concatenate([vp_ref[...], vc_ref[...]], axis=0)

    def dup_half(x2, half):
        rolled = pltpu.roll(x2, 64, axis=1)
        first = lane < 64
        return jnp.where(first, x2, rolled) if half == 0 else jnp.where(first, rolled, x2)

    for kvh in range(SWA_KV_HEADS):
        pair_cols = slice((kvh // 2) * 128, (kvh // 2 + 1) * 128)
        k2 = dup_half(kband[:, pair_cols].astype(F32), kvh % 2).astype(BF16)
        v2 = dup_half(vband[:, pair_cols].astype(F32), kvh % 2).astype(BF16)
        parts = []
        for j in range(SWA_GROUP // 2):
            c0 = kvh * SWA_GROUP * SWA_DH + j * 128
            q2 = q_ref[:, c0:c0 + 128]
            zero = jnp.zeros_like(q2)
            parts.append(jnp.where(lane_q < 64, q2, zero))
            parts.append(jnp.where(lane_q < 64, zero, q2))
        qstack = jnp.concatenate(parts, axis=0)
        s_all = _dot_nt(qstack, k2)
        ps = []
        dens = []
        for g in range(SWA_GROUP):
            hq = kvh * SWA_GROUP + g
            sg = s_all[g * blk:(g + 1) * blk] + bias_ref[0, hq]
            sink = sink_ref[hq] * LOG2E
            m = jnp.maximum(jnp.max(sg, axis=-1, keepdims=True), sink)
            p = jnp.exp2(sg - m)
            dens.append(jnp.sum(p, axis=-1, keepdims=True) + jnp.exp2(sink - m))
            ps.append(p.astype(BF16))
        o_all = _dot(jnp.concatenate(ps, axis=0), v2)
        for j in range(SWA_GROUP // 2):
            oa = o_all[(2 * j) * blk:(2 * j + 1) * blk] / dens[2 * j]
            ob = o_all[(2 * j + 1) * blk:(2 * j + 2) * blk] / dens[2 * j + 1]
            c0 = kvh * SWA_GROUP * SWA_DH + j * 128
            o_ref[:, c0:c0 + 128] = jnp.where(lane_q < 64, oa, ob).astype(o_ref.dtype)


def _swa_attention(p, bias, sinks, b, s):
    nb = s // SWA_BLOCK
    kvw = SWA_KV_HEADS * SWA_DH
    row = lambda bi, n: bi * nb + n
    prow = lambda bi, n: bi * nb + jnp.maximum(n - 1, 0)
    return pl.pallas_call(
        _swa_body,
        grid=(b, nb),
        in_specs=[
            pl.BlockSpec((SWA_BLOCK, D_MODEL), lambda bi, n: (row(bi, n), COL_QS // D_MODEL)),
            pl.BlockSpec((SWA_BLOCK, kvw), lambda bi, n: (row(bi, n), COL_KS // kvw)),
            pl.BlockSpec((SWA_BLOCK, kvw), lambda bi, n: (prow(bi, n), COL_KS // kvw)),
            pl.BlockSpec((SWA_BLOCK, kvw), lambda bi, n: (row(bi, n), COL_VS // kvw)),
            pl.BlockSpec((SWA_BLOCK, kvw), lambda bi, n: (prow(bi, n), COL_VS // kvw)),
            pl.BlockSpec((1, SWA_Q_HEADS, SWA_BLOCK, 2 * SWA_BLOCK),
                         lambda bi, n: (jnp.minimum(n, 1), 0, 0, 0)),
            pl.BlockSpec(memory_space=pltpu.SMEM),
        ],
        out_specs=pl.BlockSpec((SWA_BLOCK, D_MODEL), lambda bi, n: (row(bi, n), 0)),
        out_shape=jax.ShapeDtypeStruct((b * s, D_MODEL), BF16),
        compiler_params=_cparams(("parallel", "arbitrary")),
        name="swa_attn",
    )(p, p, p, p, p, bias, sinks)


def _merge_body(oa_ref, ob_ref, wa_ref, wb_ref, g0_ref, g1_ref, o_ref):
    ya = _dot(oa_ref[...], wa_ref[...])
    yb = _dot(ob_ref[...], wb_ref[...])
    g0 = jax.nn.sigmoid(g0_ref[...].astype(F32))
    g1 = jax.nn.sigmoid(g1_ref[...].astype(F32))
    o_ref[...] = (g0 * ya + g1 * yb).astype(o_ref.dtype)


def _merge(oa, ob, wa, wb, p, tm, tn):
    m = oa.shape[0]
    nj = D_MODEL // tn
    return pl.pallas_call(
        _merge_body,
        grid=(m // tm, nj),
        in_specs=[
            pl.BlockSpec((tm, D_MODEL), lambda i, j: (i, 0)),
            pl.BlockSpec((tm, D_MODEL), lambda i, j: (i, 0)),
            pl.BlockSpec((D_MODEL, tn), lambda i, j: (0, j)),
            pl.BlockSpec((D_MODEL, tn), lambda i, j: (0, j)),
            pl.BlockSpec((tm, tn), lambda i, j: (i, COL_GATES // tn + j)),
            pl.BlockSpec((tm, tn), lambda i, j: (i, COL_GATES // tn + nj + j)),
        ],
        out_specs=pl.BlockSpec((tm, tn), lambda i, j: (i, j)),
        out_shape=jax.ShapeDtypeStruct((m, D_MODEL), BF16),
        compiler_params=_cparams(("parallel", "arbitrary")),
        name="gated_merge",
    )(oa, ob, wa, wb, p, p)


def _out_proj_body(mg_ref, w_ref, x_ref, g_ref, o_ref):
    y = _dot(mg_ref[...], w_ref[...])
    o_ref[...] = x_ref[...] + _rms(y, g_ref[...])


def _out_proj(mg, w, x2, gain, tm):
    m = mg.shape[0]
    return pl.pallas_call(
        _out_proj_body,
        grid=(m // tm,),
        in_specs=[
            pl.BlockSpec((tm, D_MODEL), lambda i: (i, 0)),
            pl.BlockSpec((D_MODEL, D_MODEL), lambda i: (0, 0)),
            pl.BlockSpec((tm, D_MODEL), lambda i: (i, 0)),
            pl.BlockSpec((1, D_MODEL), lambda i: (0, 0)),
        ],
        out_specs=pl.BlockSpec((tm, D_MODEL), lambda i: (i, 0)),
        out_shape=jax.ShapeDtypeStruct((m, D_MODEL), F32),
        compiler_params=_cparams(("parallel",)),
        name="out_proj",
    )(mg, w, x2, gain)


HALO = 16


def _ffn_up_body(x_ref, xh_ref, g_ref, wa_ref, wb_ref, cw_ref, cb_ref, o_ref, hn_ref, a_ref, *, tiles_per_seq):
    i = pl.program_id(0)
    tm = x_ref.shape[0]

    @pl.when(pl.program_id(1) == 0)
    def _():
        halo = _rms(xh_ref[...], g_ref[...])
        halo = jnp.where(i % tiles_per_seq == 0, jnp.zeros_like(halo), halo)
        hn_ref[0:HALO, :] = halo.astype(BF16)
        hn_ref[HALO:, :] = _rms(x_ref[...], g_ref[...]).astype(BF16)

    a_ref[...] = _dot(hn_ref[...], wa_ref[...])
    gate = _dot(hn_ref[HALO:, :], wb_ref[...])
    c = cb_ref[...] + cw_ref[0:1, :] * a_ref[pl.ds(HALO - 2, tm), :]
    c = c + cw_ref[1:2, :] * a_ref[pl.ds(HALO - 1, tm), :]
    c = c + cw_ref[2:3, :] * a_ref[pl.ds(HALO, tm), :]
    o_ref[...] = (jax.nn.gelu(c, approximate=True) * gate).astype(o_ref.dtype)


def _ffn_up(x1, gain, w_up, conv_w, conv_b, s, tm, tn):
    m = x1.shape[0]
    nj = D_FF // tn
    hb = tm // HALO
    return pl.pallas_call(
        functools.partial(_ffn_up_body, tiles_per_seq=s // tm),
        grid=(m // tm, nj),
        in_specs=[
            pl.BlockSpec((tm, D_MODEL), lambda i, j: (i, 0)),
            pl.BlockSpec((HALO, D_MODEL), lambda i, j: (jnp.maximum(i * hb - 1, 0), 0)),
            pl.BlockSpec((1, D_MODEL), lambda i, j: (0, 0)),
            pl.BlockSpec((D_MODEL, tn), lambda i, j: (0, j)),
            pl.BlockSpec((D_MODEL, tn), lambda i, j: (0, nj + j)),
            pl.BlockSpec((3, tn), lambda i, j: (0, j)),
            pl.BlockSpec((1, tn), lambda i, j: (0, j)),
        ],
        out_specs=pl.BlockSpec((tm, tn), lambda i, j: (i, j)),
        out_shape=jax.ShapeDtypeStruct((m, D_FF), BF16),
        scratch_shapes=[pltpu.VMEM((HALO + tm, D_MODEL), BF16), pltpu.VMEM((HALO + tm, tn), F32)],
        compiler_params=_cparams(("parallel", "arbitrary")),
        name="ffn_up_conv",
    )(x1, x1, gain, w_up, w_up, conv_w, conv_b)


def _ffn_down_body(t_ref, w_ref, x_ref, g_ref, o_ref, acc_ref):
    kk = pl.program_id(1)

    @pl.when(kk == 0)
    def _():
        acc_ref[...] = jnp.zeros_like(acc_ref)

    acc_ref[...] += _dot(t_ref[...], w_ref[...])

    @pl.when(kk == pl.num_programs(1) - 1)
    def _():
        o_ref[...] = x_ref[...] + _rms(acc_ref[...], g_ref[...])


def _ffn_down(t, w, x1, gain, tm, tk):
    m = t.shape[0]
    return pl.pallas_call(
        _ffn_down_body,
        grid=(m // tm, D_FF // tk),
        in_specs=[
            pl.BlockSpec((tm, tk), lambda i, k: (i, k)),
            pl.BlockSpec((tk, D_MODEL), lambda i, k: (k, 0)),
            pl.BlockSpec((tm, D_MODEL), lambda i, k: (i, 0)),
            pl.BlockSpec((1, D_MODEL), lambda i, k: (0, 0)),
        ],
        out_specs=pl.BlockSpec((tm, D_MODEL), lambda i, k: (i, 0)),
        out_shape=jax.ShapeDtypeStruct((m, D_MODEL), F32),
        scratch_shapes=[pltpu.VMEM((tm, D_MODEL), F32)],
        compiler_params=_cparams(("parallel", "arbitrary")),
        name="ffn_down",
    )(t, w, x1, gain)


def _prep_w_in(w_in):
    cq = w_in[:, 0:512]
    ckv = w_in[:, 512:768]
    kr = w_in[:, 768:832]
    qs = w_in[:, 832:2880] * SWA_QSCALE
    ks = w_in[:, 2880:3136]
    vs = w_in[:, 3136:3392]
    gates = w_in[:, 3392:7488]
    z64 = jnp.zeros((D_MODEL, 64), BF16)
    kr_sw = jnp.concatenate([kr[:, 32:], kr[:, :32]], axis=1)
    parts = [gates, qs, cq, ks, vs, ckv, kr, z64, kr_sw, z64]
    return jnp.concatenate([x.astype(BF16) for x in parts], axis=1)


def _prep_w_q_up(w):
    w = w.reshape(MLA_Q_RANK, MLA_HEADS, MLA_NOPE + MLA_ROPE)
    nope = w[:, :, :MLA_NOPE]
    rope = w[:, :, MLA_NOPE:]
    rope_sw = jnp.concatenate([rope[:, :, 32:], rope[:, :, :32]], axis=2)
    wq = jnp.concatenate([nope, rope, rope_sw], axis=2)
    return jnp.transpose(wq, (1, 2, 0)).astype(BF16)


def _prep_w_kv_up(w):
    w = w.reshape(MLA_KV_RANK, MLA_HEADS, MLA_NOPE + MLA_V)
    wk = w[:, :, :MLA_NOPE].reshape(MLA_KV_RANK, MLA_HEADS * MLA_NOPE).astype(BF16)
    wvt = jnp.transpose(w[:, :, MLA_NOPE:], (1, 2, 0)).astype(BF16)
    return wk, wvt


def _rope_tables(s):
    inv = ROPE_THETA ** (-jnp.arange(0, MLA_ROPE, 2, dtype=F32) / MLA_ROPE)
    ang = jnp.arange(s).astype(F32)[:, None] * inv[None, :]
    cos, sin = jnp.cos(ang), jnp.sin(ang)
    z = jnp.zeros((s, 64), F32)
    ck = jnp.concatenate([cos, cos, z], axis=1)
    sk = jnp.concatenate([-sin, sin, z], axis=1)
    cq2 = jnp.concatenate([cos, cos], axis=1).T
    sq2 = jnp.concatenate([-sin, sin], axis=1).T
    return cq2, sq2, ck, sk


def _tile(n, pref):
    t = min(n, pref)
    assert n % t == 0, (n, pref)
    return t


def kernel(x, norm_mix_pre, norm_mix_post, norm_ffn_pre, norm_ffn_post, w_in, mla_q_norm, mla_w_q_up, mla_kv_norm, mla_w_kv_up, swa_sinks, rel_bias_table, w_o_mla, w_o_swa, w_out, ffn_w_up, ffn_conv_w, ffn_conv_b, ffn_w_down):
    b, s, d = x.shape
    assert d == D_MODEL and s % SWA_BLOCK == 0
    depth = w_in.shape[0]
    m = b * s
    x2 = x.reshape(m, d)
    cq2, sq2, ck, sk = _rope_tables(s)
    bias = _swa_bias(rel_bias_table.astype(F32))
    for l in range(depth):
        p = _in_proj(x2, norm_mix_pre[l][None], _prep_w_in(w_in[l]), _tile(m, 1024), 1280)
        wk, wvt = _prep_w_kv_up(mla_w_kv_up[l])
        qt, k, vt = _qkv_up(p, mla_q_norm[l][None], mla_kv_norm[l][None], _prep_w_q_up(mla_w_q_up[l]),
                            wk, wvt, cq2, sq2, ck, sk, b, s, _tile(s, 512))
        o_a = _mla_attention(qt, k, vt, _tile(s, 1024)).reshape(m, d)
        o_b = _swa_attention(p, bias, swa_sinks[l].astype(F32), b, s)
        mg = _merge(o_a, o_b, w_o_mla[l].astype(BF16), w_o_swa[l].astype(BF16), p, _tile(m, 512), 1024)
        x1 = _out_proj(mg, w_out[l].astype(BF16), x2, norm_mix_post[l][None], _tile(m, 512))
        t = _ffn_up(x1, norm_ffn_pre[l][None], ffn_w_up[l].astype(BF16), ffn_conv_w[l], ffn_conv_b[l][None],
                    s, _tile(s, 512), 1408)
        x2 = _ffn_down(t, ffn_w_down[l].astype(BF16), x1, norm_ffn_post[l][None], _tile(m, 512), 1408)
    return x2.reshape(b, s, d)
```

```python
import functools
import math

import numpy as np
import jax
import jax.numpy as jnp
from jax import lax
from jax.experimental import pallas as pl
from jax.experimental.pallas import tpu as pltpu

F32 = jnp.float32
BF16 = jnp.bfloat16

D_MODEL = 2048
MLA_HEADS = 16
MLA_Q_RANK = 512
MLA_KV_RANK = 256
MLA_NOPE = 128
MLA_ROPE = 64
MLA_V = 128
MLA_QK_PAD = 256
ROPE_THETA = 10000.0
SWA_Q_HEADS = 32
SWA_KV_HEADS = 4
SWA_GROUP = 8
SWA_DH = 64
SWA_BLOCK = 128
REL_BUCKETS = 32
REL_MAX_DIST = 128
D_FF = 5632
EPS = 1e-6
NEG = -1e30
LOG2E = 1.4426950408889634
SWA_QSCALE = SWA_DH ** -0.5 * LOG2E

COL_GATES = 0
COL_QS = 4096
COL_CQ = 6144
COL_KS = 6656
COL_VS = 6912
COL_CKV = 7168
COL_KRA = 7424
COL_KRB = 7552
IN_COLS_PAD = 7680

VMEM_LIMIT = 56 * 1024 * 1024


def _cparams(sem, flags=None):
    return pltpu.CompilerParams(dimension_semantics=sem, vmem_limit_bytes=VMEM_LIMIT, flags=flags)


def _rms(x, g):
    return x * lax.rsqrt(jnp.mean(x * x, axis=-1, keepdims=True) + EPS) * g


def _dot(a, b):
    return jnp.dot(a, b, preferred_element_type=F32)


def _dot_nt(a, b):
    return lax.dot_general(a, b, (((1,), (1,)), ((), ())), preferred_element_type=F32)


def _in_proj_body(x_ref, g_ref, w_ref, o_ref, hn_ref):
    @pl.when(pl.program_id(1) == 0)
    def _():
        hn_ref[...] = _rms(x_ref[...], g_ref[...]).astype(BF16)

    o_ref[...] = _dot(hn_ref[...], w_ref[...]).astype(o_ref.dtype)


def _in_proj(x2, gain, w, tm, tn):
    m = x2.shape[0]
    n = w.shape[1]
    return pl.pallas_call(
        _in_proj_body,
        grid=(m // tm, n // tn),
        in_specs=[
            pl.BlockSpec((tm, D_MODEL), lambda i, j: (i, 0)),
            pl.BlockSpec((1, D_MODEL), lambda i, j: (0, 0)),
            pl.BlockSpec((D_MODEL, tn), lambda i, j: (0, j)),
        ],
        out_specs=pl.BlockSpec((tm, tn), lambda i, j: (i, j)),
        out_shape=jax.ShapeDtypeStruct((m, n), BF16),
        scratch_shapes=[pltpu.VMEM((tm, D_MODEL), BF16)],
        compiler_params=_cparams(("parallel", "arbitrary")),
        name="in_proj",
    )(x2, gain, w)


def _qkv_up_body(cq_ref, ckv_ref, kra_ref, krb_ref, gq_ref, gkv_ref, wqt_ref, wk_ref, wvt_ref,
                 cq2_ref, sq2_ref, ck_ref, sk_ref, qt_ref, k_ref, vt_ref, qn2_ref, kn2_ref, *, qscale):
    ts = cq_ref.shape[0]
    cqn = _rms(cq_ref[...].astype(F32), gq_ref[...]).astype(BF16)
    ckvn = _rms(ckv_ref[...].astype(F32), gkv_ref[...]).astype(BF16)
    cq2 = cq2_ref[...]
    sq2 = sq2_ref[...]
    zeros_q = jnp.zeros((MLA_QK_PAD - MLA_NOPE - MLA_ROPE, ts), BF16)
    ones_v = jnp.ones((MLA_ACC_ROWS - MLA_V, ts), BF16)
    ones_n = jnp.ones((8, 128), BF16)
    for h in range(MLA_HEADS):
        qf = _dot_nt(wqt_ref[h], cqn) * qscale
        rope = qf[MLA_NOPE:MLA_NOPE + MLA_ROPE] * cq2 + qf[MLA_NOPE + MLA_ROPE:] * sq2
        qt_ref[0, h, 0:MLA_NOPE, :] = qf[0:MLA_NOPE].astype(BF16)
        qt_ref[0, h, MLA_NOPE:MLA_NOPE + MLA_ROPE, :] = rope.astype(BF16)
        qt_ref[0, h, MLA_NOPE + MLA_ROPE:, :] = zeros_q
        qn2 = (jnp.sum(qf[0:MLA_NOPE] * qf[0:MLA_NOPE], axis=0, keepdims=True)
               + jnp.sum(rope * rope, axis=0, keepdims=True))
        qn2_ref[0, h] = jnp.broadcast_to(qn2, (8, ts))
    kr = kra_ref[...].astype(F32) * ck_ref[...] + krb_ref[...].astype(F32) * sk_ref[...]
    lane = lax.broadcasted_iota(jnp.int32, kr.shape, 1)
    kr_aug = jnp.where(lane == MLA_ROPE, 1.0, kr).astype(BF16)
    kall = _dot(ckvn, wk_ref[...])
    for h in range(MLA_HEADS):
        kh = kall[:, h * MLA_NOPE:(h + 1) * MLA_NOPE]
        k_ref[0, h, :, 0:MLA_NOPE] = kh.astype(BF16)
        k_ref[0, h, :, MLA_NOPE:] = kr_aug
        kn2_ref[0, h] = _dot_nt(ones_n, (kh * kh + kr * kr).astype(BF16))
        vt_ref[0, h, 0:MLA_V, :] = _dot_nt(wvt_ref[h], ckvn).astype(BF16)
        vt_ref[0, h, MLA_V:, :] = ones_v


def _qkv_up(p, gq, gkv, wqt, wk, wvt, cq2, sq2, ck, sk, b, s, ts):
    ns = s // ts
    row = lambda bi, i: bi * ns + i
    qscale = float((MLA_NOPE + MLA_ROPE) ** -0.5 * LOG2E)
    full = lambda shape: pl.BlockSpec(shape, lambda bi, i: (0,) * len(shape))
    return pl.pallas_call(
        functools.partial(_qkv_up_body, qscale=qscale),
        grid=(b, ns),
        in_specs=[
            pl.BlockSpec((ts, MLA_Q_RANK), lambda bi, i: (row(bi, i), COL_CQ // MLA_Q_RANK)),
            pl.BlockSpec((ts, MLA_KV_RANK), lambda bi, i: (row(bi, i), COL_CKV // MLA_KV_RANK)),
            pl.BlockSpec((ts, 128), lambda bi, i: (row(bi, i), COL_KRA // 128)),
            pl.BlockSpec((ts, 128), lambda bi, i: (row(bi, i), COL_KRB // 128)),
            full((1, MLA_Q_RANK)),
            full((1, MLA_KV_RANK)),
            full((MLA_HEADS, MLA_QK_PAD, MLA_Q_RANK)),
            full((MLA_KV_RANK, MLA_HEADS * MLA_NOPE)),
            full((MLA_HEADS, MLA_V, MLA_KV_RANK)),
            pl.BlockSpec((MLA_ROPE, ts), lambda bi, i: (0, i)),
            pl.BlockSpec((MLA_ROPE, ts), lambda bi, i: (0, i)),
            pl.BlockSpec((ts, 128), lambda bi, i: (i, 0)),
            pl.BlockSpec((ts, 128), lambda bi, i: (i, 0)),
        ],
        out_specs=[
            pl.BlockSpec((1, MLA_HEADS, MLA_QK_PAD, ts), lambda bi, i: (bi, 0, 0, i)),
            pl.BlockSpec((1, MLA_HEADS, ts, MLA_QK_PAD), lambda bi, i: (bi, 0, i, 0)),
            pl.BlockSpec((1, MLA_HEADS, MLA_ACC_ROWS, ts), lambda bi, i: (bi, 0, 0, i)),
            pl.BlockSpec((1, MLA_HEADS, 8, ts), lambda bi, i: (bi, 0, 0, i)),
            pl.BlockSpec((1, MLA_HEADS, 8, ts), lambda bi, i: (bi, 0, 0, i)),
        ],
        out_shape=[
            jax.ShapeDtypeStruct((b, MLA_HEADS, MLA_QK_PAD, s), BF16),
            jax.ShapeDtypeStruct((b, MLA_HEADS, s, MLA_QK_PAD), BF16),
            jax.ShapeDtypeStruct((b, MLA_HEADS, MLA_ACC_ROWS, s), BF16),
            jax.ShapeDtypeStruct((b, MLA_HEADS, 8, s), F32),
            jax.ShapeDtypeStruct((b, MLA_HEADS, 8, s), F32),
        ],
        compiler_params=_cparams(("parallel", "parallel")),
        name="qkv_up",
    )(p, p, p, p, gq, gkv, wqt, wk, wvt, cq2, sq2, ck, sk)


MLA_GW = 256
MLA_QK_ROWS = 512
MLA_ACC_ROWS = 144
MLA_SHIFT_ROW = MLA_NOPE + MLA_ROPE
MLA_UB_MARGIN = 1.03125
MLA_L_MIN = 2.0 ** -100


def _mla_body(qt_ref, k_ref, vt_ref, qn2_ref, kn2_ref, o_ref, acc_ref, qs_ref, m_ref, p0, p1, *, t):
    i = pl.program_id(2)
    n_groups = t // MLA_GW
    p_s = (p0, p1)

    def gcols(g):
        return slice(g * MLA_GW, (g + 1) * MLA_GW)

    kmax2 = jnp.max(jnp.max(kn2_ref[0, 0], axis=1, keepdims=True), axis=0, keepdims=True)
    ub = jnp.sqrt(qn2_ref[0, 0, 0:1, :] * kmax2) * MLA_UB_MARGIN
    qs_ref[...] = qt_ref[0, 0]
    row = lax.broadcasted_iota(jnp.int32, (16, t), 0)
    qs_ref[MLA_SHIFT_ROW:MLA_SHIFT_ROW + 16, :] = jnp.where(row == 0, -ub, 0.0).astype(BF16)
    acc_ref[...] = jnp.zeros_like(acc_ref)

    def probs(k0, rows, g, masked):
        ps = []
        for r in range(0, rows, MLA_QK_ROWS):
            n = min(MLA_QK_ROWS, rows - r)
            st = _dot(k_ref[0, 0, pl.ds(k0 + r, n), :], qs_ref[:, gcols(g)])
            if masked and r + n > rows - MLA_GW:
                kpos = r + lax.broadcasted_iota(jnp.int32, (n, MLA_GW), 0)
                qpos = rows - MLA_GW + lax.broadcasted_iota(jnp.int32, (n, MLA_GW), 1)
                st = jnp.where(kpos <= qpos, st, NEG)
            ps.append(jnp.exp2(st).astype(BF16))
        return ps[0] if len(ps) == 1 else jnp.concatenate(ps, axis=0)

    def key_off(j):
        return pl.multiple_of(jnp.where(j == 0, i, j - 1) * t, t)

    def pv(j, slot, g):
        acc_ref[g] += _dot(vt_ref[0, 0, :, pl.ds(key_off(j), t)], p_s[slot][g])

    def step(k, slot):
        for g in range(n_groups):
            p_s[slot][g] = probs(key_off(k), t, g, False)
            pv(k - 1, 1 - slot, g)

    for g in range(n_groups):
        rows = (g + 1) * MLA_GW
        p0[g, 0:rows, :] = probs(pl.multiple_of(i * t, t), rows, g, True)
        if rows < t:
            p0[g, rows:, :] = jnp.zeros((t - rows, MLA_GW), BF16)

    def pair(u, carry):
        step(2 * u + 1, 1)
        step(2 * u + 2, 0)
        return carry

    lax.fori_loop(0, i // 2, pair, 0)

    @pl.when(i % 2 == 1)
    def _():
        step(i, 1)
        for g in range(n_groups):
            pv(i, 1, g)

    @pl.when(i % 2 == 0)
    def _():
        for g in range(n_groups):
            pv(i, 0, g)

    l_min = acc_ref[0, MLA_V:MLA_V + 1, :]
    for g in range(1, n_groups):
        l_min = jnp.minimum(l_min, acc_ref[g, MLA_V:MLA_V + 1, :])
    l_min = jnp.min(l_min, axis=1, keepdims=True)

    @pl.when(jnp.logical_not(l_min[0, 0] >= MLA_L_MIN))
    def _():
        acc_ref[...] = jnp.zeros_like(acc_ref)
        m_ref[...] = jnp.full_like(m_ref, NEG)

        def exact_chunk(c, carry):
            k0 = pl.multiple_of(c * MLA_QK_ROWS, MLA_QK_ROWS)
            kc = k_ref[0, 0, pl.ds(k0, MLA_QK_ROWS), :]
            vt1 = vt_ref[0, 0, :, pl.ds(k0, MLA_QK_ROWS)]
            for g in range(n_groups):
                st = _dot(kc, qt_ref[0, 0, :, gcols(g)])
                kpos = k0 + lax.broadcasted_iota(jnp.int32, st.shape, 0)
                qpos = i * t + g * MLA_GW + lax.broadcasted_iota(jnp.int32, st.shape, 1)
                st = jnp.where(kpos <= qpos, st, NEG)
                m_old = m_ref[g]
                m_new = jnp.maximum(m_old, jnp.max(st, axis=0, keepdims=True))
                m_ref[g] = m_new
                p = jnp.exp2(st - m_new).astype(BF16)
                acc_ref[g] = jnp.exp2(m_old - m_new) * acc_ref[g] + _dot(vt1, p)
            return carry

        lax.fori_loop(0, (i + 1) * (t // MLA_QK_ROWS), exact_chunk, 0)

    for g in range(n_groups):
        out_t = acc_ref[g, 0:MLA_V, :] / acc_ref[g, MLA_V:MLA_V + 1, :]
        o_ref[0, g * MLA_GW:(g + 1) * MLA_GW, :] = out_t.T.astype(o_ref.dtype)


def _mla_attention(qt, k, vt, qn2, kn2, t):
    b, h, _, s = qt.shape
    n_groups = t // MLA_GW
    return pl.pallas_call(
        functools.partial(_mla_body, t=t),
        grid=(b, h, s // t),
        in_specs=[
            pl.BlockSpec((1, 1, MLA_QK_PAD, t), lambda bi, hi, i: (bi, hi, 0, i)),
            pl.BlockSpec((1, 1, s, MLA_QK_PAD), lambda bi, hi, i: (bi, hi, 0, 0)),
            pl.BlockSpec((1, 1, MLA_ACC_ROWS, s), lambda bi, hi, i: (bi, hi, 0, 0)),
            pl.BlockSpec((1, 1, 8, t), lambda bi, hi, i: (bi, hi, 0, i)),
            pl.BlockSpec((1, 1, 8, s), lambda bi, hi, i: (bi, hi, 0, 0)),
        ],
        out_specs=pl.BlockSpec((1, t, MLA_V), lambda bi, hi, i: (bi, i, hi)),
        out_shape=jax.ShapeDtypeStruct((b, s, h * MLA_V), BF16),
        scratch_shapes=[
            pltpu.VMEM((n_groups, MLA_ACC_ROWS, MLA_GW), F32),
            pltpu.VMEM((MLA_QK_PAD, t), BF16),
            pltpu.VMEM((n_groups, 1, MLA_GW), F32),
            pltpu.VMEM((n_groups, t, MLA_GW), BF16),
            pltpu.VMEM((n_groups, t, MLA_GW), BF16),
        ],
        compiler_params=_cparams(("parallel", "parallel", "arbitrary")),
        name="mla_attn",
    )(qt, k, vt, qn2, kn2)


def _t5_bucket_table():
    n = np.arange(REL_MAX_DIST)
    max_exact = REL_BUCKETS // 2
    large = max_exact + (np.log(np.maximum(n, 1).astype(np.float32) / max_exact)
                         / math.log(REL_MAX_DIST / max_exact) * (REL_BUCKETS - max_exact)).astype(np.int32)
    large = np.minimum(large, REL_BUCKETS - 1)
    return np.where(n < max_exact, n, large)


def _swa_bias_body(bucket_ref, table_ref, o_ref):
    bucket = bucket_ref[...]
    kj = lax.broadcasted_iota(jnp.int32, bucket.shape, 1)
    for h in range(SWA_Q_HEADS):
        bias = jnp.zeros(bucket.shape, F32)
        for bk in range(REL_BUCKETS):
            bias = jnp.where(bucket == bk, table_ref[bk, h], bias)
        valid = bucket >= 0
        bias = bias * LOG2E
        o_ref[1, h] = jnp.where(valid, bias, NEG)
        o_ref[0, h] = jnp.where(valid & (kj >= SWA_BLOCK), bias, NEG)


def _swa_bias(rel_table):
    qi = np.arange(SWA_BLOCK)[:, None]
    kj = np.arange(2 * SWA_BLOCK)[None, :]
    dist = qi + SWA_BLOCK - kj
    in_window = (dist >= 0) & (dist < SWA_BLOCK)
    bucket = np.where(in_window, _t5_bucket_table()[np.clip(dist, 0, REL_MAX_DIST - 1)], -1).astype(np.int32)
    return pl.pallas_call(
        _swa_bias_body,
        in_specs=[
            pl.BlockSpec(memory_space=pltpu.VMEM),
            pl.BlockSpec(memory_space=pltpu.SMEM),
        ],
        out_specs=pl.BlockSpec(memory_space=pltpu.VMEM),
        out_shape=jax.ShapeDtypeStruct((2, SWA_Q_HEADS, SWA_BLOCK, 2 * SWA_BLOCK), F32),
        name="swa_bias",
    )(jnp.asarray(bucket), rel_table)


def _swa_body(q_ref, kc_ref, kp_ref, vc_ref, vp_ref, bias_ref, sink_ref, o_ref):
    blk = SWA_BLOCK
    lane = lax.broadcasted_iota(jnp.int32, (2 * blk, 128), 1)
    lane_q = lax.broadcasted_iota(jnp.int32, (blk, 128), 1)
    kband = jnp.concatenate([kp_ref[...], kc_ref[...]], axis=0)
    vband = jnp.concatenate([vp_ref[...], vc_ref[...]], axis=0)

    def dup_half(x2, half):
        rolled = pltpu.roll(x2, 64, axis=1)
        first = lane < 64
        return jnp.where(first, x2, rolled) if half == 0 else jnp.where(first, rolled, x2)

    for kvh in range(SWA_KV_HEADS):
        pair_cols = slice((kvh // 2) * 128, (kvh // 2 + 1) * 128)
        k2 = dup_half(kband[:, pair_cols].astype(F32), kvh % 2).astype(BF16)
        v2 = dup_half(vband[:, pair_cols].astype(F32), kvh % 2).astype(BF16)
        parts = []
        for j in range(SWA_GROUP // 2):
            c0 = kvh * SWA_GROUP * SWA_DH + j * 128
            q2 = q_ref[:, c0:c0 + 128]
            zero = jnp.zeros_like(q2)
            parts.append(jnp.where(lane_q < 64, q2, zero))
            parts.append(jnp.where(lane_q < 64, zero, q2))
        qstack = jnp.concatenate(parts, axis=0)
        s_all = _dot_nt(qstack, k2)
        ps = []
        dens = []
        for g in range(SWA_GROUP):
            hq = kvh * SWA_GROUP + g
            sg = s_all[g * blk:(g + 1) * blk] + bias_ref[0, hq]
            sink = sink_ref[hq] * LOG2E
            m = jnp.maximum(jnp.max(sg, axis=-1, keepdims=True), sink)
            p = jnp.exp2(sg - m)
            dens.append(jnp.sum(p, axis=-1, keepdims=True) + jnp.exp2(sink - m))
            ps.append(p.astype(BF16))
        o_all = _dot(jnp.concatenate(ps, axis=0), v2)
        for j in range(SWA_GROUP // 2):
            oa = o_all[(2 * j) * blk:(2 * j + 1) * blk] / dens[2 * j]
            ob = o_all[(2 * j + 1) * blk:(2 * j + 2) * blk] / dens[2 * j + 1]
            c0 = kvh * SWA_GROUP * SWA_DH + j * 128
            o_ref[:, c0:c0 + 128] = jnp.where(lane_q < 64, oa, ob).astype(o_ref.dtype)


def _swa_attention(p, bias, sinks, b, s):
    nb = s // SWA_BLOCK
    kvw = SWA_KV_HEADS * SWA_DH
    row = lambda bi, n: bi * nb + n
    prow = lambda bi, n: bi * nb + jnp.maximum(n - 1, 0)
    return pl.pallas_call(
        _swa_body,
        grid=(b, nb),
        in_specs=[
            pl.BlockSpec((SWA_BLOCK, D_MODEL), lambda bi, n: (row(bi, n), COL_QS // D_MODEL)),
            pl.BlockSpec((SWA_BLOCK, kvw), lambda bi, n: (row(bi, n), COL_KS // kvw)),
            pl.BlockSpec((SWA_BLOCK, kvw), lambda bi, n: (prow(bi, n), COL_KS // kvw)),
            pl.BlockSpec((SWA_BLOCK, kvw), lambda bi, n: (row(bi, n), COL_VS // kvw)),
            pl.BlockSpec((SWA_BLOCK, kvw), lambda bi, n: (prow(bi, n), COL_VS // kvw)),
            pl.BlockSpec((1, SWA_Q_HEADS, SWA_BLOCK, 2 * SWA_BLOCK),
                         lambda bi, n: (jnp.minimum(n, 1), 0, 0, 0)),
            pl.BlockSpec(memory_space=pltpu.SMEM),
        ],
        out_specs=pl.BlockSpec((SWA_BLOCK, D_MODEL), lambda bi, n: (row(bi, n), 0)),
        out_shape=jax.ShapeDtypeStruct((b * s, D_MODEL), BF16),
        compiler_params=_cparams(("parallel", "arbitrary")),
        name="swa_attn",
    )(p, p, p, p, p, bias, sinks)


def _merge_body(oa_ref, ob_ref, wa_ref, wb_ref, g0_ref, g1_ref, o_ref):
    ya = _dot(oa_ref[...], wa_ref[...])
    yb = _dot(ob_ref[...], wb_ref[...])
    g0 = jax.nn.sigmoid(g0_ref[...].astype(F32))
    g1 = jax.nn.sigmoid(g1_ref[...].astype(F32))
    o_ref[...] = (g0 * ya + g1 * yb).astype(o_ref.dtype)


def _merge(oa, ob, wa, wb, p, tm, tn):
    m = oa.shape[0]
    nj = D_MODEL // tn
    return pl.pallas_call(
        _merge_body,
        grid=(m // tm, nj),
        in_specs=[
            pl.BlockSpec((tm, D_MODEL), lambda i, j: (i, 0)),
            pl.BlockSpec((tm, D_MODEL), lambda i, j: (i, 0)),
            pl.BlockSpec((D_MODEL, tn), lambda i, j: (0, j)),
            pl.BlockSpec((D_MODEL, tn), lambda i, j: (0, j)),
            pl.BlockSpec((tm, tn), lambda i, j: (i, COL_GATES // tn + j)),
            pl.BlockSpec((tm, tn), lambda i, j: (i, COL_GATES // tn + nj + j)),
        ],
        out_specs=pl.BlockSpec((tm, tn), lambda i, j: (i, j)),
        out_shape=jax.ShapeDtypeStruct((m, D_MODEL), BF16),
        compiler_params=_cparams(("parallel", "arbitrary")),
        name="gated_merge",
    )(oa, ob, wa, wb, p, p)


def _out_proj_body(mg_ref, w_ref, x_ref, g_ref, o_ref):
    y = _dot(mg_ref[...], w_ref[...])
    o_ref[...] = x_ref[...] + _rms(y, g_ref[...])


def _out_proj(mg, w, x2, gain, tm):
    m = mg.shape[0]
    return pl.pallas_call(
        _out_proj_body,
        grid=(m // tm,),
        in_specs=[
            pl.BlockSpec((tm, D_MODEL), lambda i: (i, 0)),
            pl.BlockSpec((D_MODEL, D_MODEL), lambda i: (0, 0)),
            pl.BlockSpec((tm, D_MODEL), lambda i: (i, 0)),
            pl.BlockSpec((1, D_MODEL), lambda i: (0, 0)),
        ],
        out_specs=pl.BlockSpec((tm, D_MODEL), lambda i: (i, 0)),
        out_shape=jax.ShapeDtypeStruct((m, D_MODEL), F32),
        compiler_params=_cparams(("parallel",)),
        name="out_proj",
    )(mg, w, x2, gain)


HALO = 16


def _ffn_body(x_ref, xh_ref, gpre_ref, wa_ref, wb_ref, cw_ref, cb_ref, wd_ref, gpost_ref, o_ref,
              hn_ref, a_ref, acc_ref, *, tiles_per_seq):
    i = pl.program_id(0)
    j = pl.program_id(1)
    tm = x_ref.shape[0]

    @pl.when(j == 0)
    def _():
        halo = _rms(xh_ref[...], gpre_ref[...])
        halo = jnp.where(i % tiles_per_seq == 0, jnp.zeros_like(halo), halo)
        hn_ref[0:HALO, :] = halo.astype(BF16)
        hn_ref[HALO:, :] = _rms(x_ref[...], gpre_ref[...]).astype(BF16)
        acc_ref[...] = jnp.zeros_like(acc_ref)

    a_ref[...] = _dot(hn_ref[...], wa_ref[...])
    gate = _dot(hn_ref[HALO:, :], wb_ref[...])
    c = cb_ref[...] + cw_ref[0:1, :] * a_ref[pl.ds(HALO - 2, tm), :]
    c = c + cw_ref[1:2, :] * a_ref[pl.ds(HALO - 1, tm), :]
    c = c + cw_ref[2:3, :] * a_ref[pl.ds(HALO, tm), :]
    act = (jax.nn.gelu(c, approximate=True) * gate).astype(BF16)
    acc_ref[...] += _dot(act, wd_ref[...])

    @pl.when(j == pl.num_programs(1) - 1)
    def _():
        o_ref[...] = x_ref[...] + _rms(acc_ref[...], gpost_ref[...])


def _ffn(x1, gain_pre, w_up, conv_w, conv_b, w_down, gain_post, s, tm, tn):
    m = x1.shape[0]
    nj = D_FF // tn
    hb = tm // HALO
    return pl.pallas_call(
        functools.partial(_ffn_body, tiles_per_seq=s // tm),
        grid=(m // tm, nj),
        in_specs=[
            pl.BlockSpec((tm, D_MODEL), lambda i, j: (i, 0)),
            pl.BlockSpec((HALO, D_MODEL), lambda i, j: (jnp.maximum(i * hb - 1, 0), 0)),
            pl.BlockSpec((1, D_MODEL), lambda i, j: (0, 0)),
            pl.BlockSpec((D_MODEL, tn), lambda i, j: (0, j)),
            pl.BlockSpec((D_MODEL, tn), lambda i, j: (0, nj + j)),
            pl.BlockSpec((3, tn), lambda i, j: (0, j)),
            pl.BlockSpec((1, tn), lambda i, j: (0, j)),
            pl.BlockSpec((tn, D_MODEL), lambda i, j: (j, 0)),
            pl.BlockSpec((1, D_MODEL), lambda i, j: (0, 0)),
        ],
        out_specs=pl.BlockSpec((tm, D_MODEL), lambda i, j: (i, 0)),
        out_shape=jax.ShapeDtypeStruct((m, D_MODEL), F32),
        scratch_shapes=[
            pltpu.VMEM((HALO + tm, D_MODEL), BF16),
            pltpu.VMEM((HALO + tm, tn), F32),
            pltpu.VMEM((tm, D_MODEL), F32),
        ],
        compiler_params=_cparams(("parallel", "arbitrary")),
        name="ffn_fused",
    )(x1, x1, gain_pre, w_up, w_up, conv_w, conv_b, w_down, gain_post)


def _prep_w_in(w_in):
    cq = w_in[:, 0:512]
    ckv = w_in[:, 512:768]
    kr = w_in[:, 768:832]
    qs = w_in[:, 832:2880] * SWA_QSCALE
    ks = w_in[:, 2880:3136]
    vs = w_in[:, 3136:3392]
    gates = w_in[:, 3392:7488]
    z64 = jnp.zeros((D_MODEL, 64), BF16)
    kr_sw = jnp.concatenate([kr[:, 32:], kr[:, :32]], axis=1)
    parts = [gates, qs, cq, ks, vs, ckv, kr, z64, kr_sw, z64]
    return jnp.concatenate([x.astype(BF16) for x in parts], axis=1)


def _prep_w_q_up(w):
    w = w.reshape(MLA_Q_RANK, MLA_HEADS, MLA_NOPE + MLA_ROPE)
    nope = w[:, :, :MLA_NOPE]
    rope = w[:, :, MLA_NOPE:]
    rope_sw = jnp.concatenate([rope[:, :, 32:], rope[:, :, :32]], axis=2)
    wq = jnp.concatenate([nope, rope, rope_sw], axis=2)
    return jnp.transpose(wq, (1, 2, 0)).astype(BF16)


def _prep_w_kv_up(w):
    w = w.reshape(MLA_KV_RANK, MLA_HEADS, MLA_NOPE + MLA_V)
    wk = w[:, :, :MLA_NOPE].reshape(MLA_KV_RANK, MLA_HEADS * MLA_NOPE).astype(BF16)
    wvt = jnp.transpose(w[:, :, MLA_NOPE:], (1, 2, 0)).astype(BF16)
    return wk, wvt


def _rope_tables(s):
    inv = ROPE_THETA ** (-jnp.arange(0, MLA_ROPE, 2, dtype=F32) / MLA_ROPE)
    ang = jnp.arange(s).astype(F32)[:, None] * inv[None, :]
    cos, sin = jnp.cos(ang), jnp.sin(ang)
    z = jnp.zeros((s, 64), F32)
    ck = jnp.concatenate([cos, cos, z], axis=1)
    sk = jnp.concatenate([-sin, sin, z], axis=1)
    cq2 = jnp.concatenate([cos, cos], axis=1).T
    sq2 = jnp.concatenate([-sin, sin], axis=1).T
    return cq2, sq2, ck, sk


def _tile(n, pref):
    t = min(n, pref)
    assert n % t == 0, (n, pref)
    return t


def kernel(x, norm_mix_pre, norm_mix_post, norm_ffn_pre, norm_ffn_post, w_in, mla_q_norm, mla_w_q_up, mla_kv_norm, mla_w_kv_up, swa_sinks, rel_bias_table, w_o_mla, w_o_swa, w_out, ffn_w_up, ffn_conv_w, ffn_conv_b, ffn_w_down):
    b, s, d = x.shape
    assert d == D_MODEL and s % SWA_BLOCK == 0
    depth = w_in.shape[0]
    m = b * s
    x2 = x.reshape(m, d)
    cq2, sq2, ck, sk = _rope_tables(s)
    bias = _swa_bias(rel_bias_table.astype(F32))
    for l in range(depth):
        p = _in_proj(x2, norm_mix_pre[l][None], _prep_w_in(w_in[l]), _tile(m, 1024), 1280)
        wk, wvt = _prep_w_kv_up(mla_w_kv_up[l])
        qt, k, vt, qn2, kn2 = _qkv_up(p, mla_q_norm[l][None], mla_kv_norm[l][None], _prep_w_q_up(mla_w_q_up[l]),
                            wk, wvt, cq2, sq2, ck, sk, b, s, _tile(s, 512))
        o_a = _mla_attention(qt, k, vt, qn2, kn2, _tile(s, 1024)).reshape(m, d)
        o_b = _swa_attention(p, bias, swa_sinks[l].astype(F32), b, s)
        mg = _merge(o_a, o_b, w_o_mla[l].astype(BF16), w_o_swa[l].astype(BF16), p, _tile(m, 512), 1024)
        x1 = _out_proj(mg, w_out[l].astype(BF16), x2, norm_mix_post[l][None], _tile(m, 512))
        x2 = _ffn(x1, norm_ffn_pre[l][None], ffn_w_up[l].astype(BF16), ffn_conv_w[l], ffn_conv_b[l][None],
                  ffn_w_down[l].astype(BF16), norm_ffn_post[l][None], s, _tile(s, 512), 512)
    return x2.reshape(b, s, d)
```

```python
import functools
import math

import numpy as np
import jax
import jax.numpy as jnp
from jax import lax
from jax.experimental import pallas as pl
from jax.experimental.pallas import tpu as pltpu

F32 = jnp.float32
BF16 = jnp.bfloat16

D_MODEL = 2048
MLA_HEADS = 16
MLA_Q_RANK = 512
MLA_KV_RANK = 256
MLA_NOPE = 128
MLA_ROPE = 64
MLA_V = 128
MLA_QK_PAD = 256
ROPE_THETA = 10000.0
SWA_Q_HEADS = 32
SWA_KV_HEADS = 4
SWA_GROUP = 8
SWA_DH = 64
SWA_BLOCK = 128
REL_BUCKETS = 32
REL_MAX_DIST = 128
D_FF = 5632
EPS = 1e-6
NEG = -1e30
LOG2E = 1.4426950408889634
SWA_QSCALE = SWA_DH ** -0.5 * LOG2E

COL_GATES = 0
COL_QS = 4096
COL_CQ = 6144
COL_KS = 6656
COL_VS = 6912
COL_CKV = 7168
COL_KRA = 7424
COL_KRB = 7552
IN_COLS_PAD = 7680

VMEM_LIMIT = 56 * 1024 * 1024


def _cparams(sem, flags=None):
    return pltpu.CompilerParams(dimension_semantics=sem, vmem_limit_bytes=VMEM_LIMIT, flags=flags)


def _rms(x, g):
    return x * lax.rsqrt(jnp.mean(x * x, axis=-1, keepdims=True) + EPS) * g


def _dot(a, b):
    return jnp.dot(a, b, preferred_element_type=F32)


def _dot_nt(a, b):
    return lax.dot_general(a, b, (((1,), (1,)), ((), ())), preferred_element_type=F32)


def _in_proj_body(x_ref, g_ref, w_ref, o_ref, hn_ref):
    @pl.when(pl.program_id(1) == 0)
    def _():
        hn_ref[...] = _rms(x_ref[...], g_ref[...]).astype(BF16)

    o_ref[...] = _dot(hn_ref[...], w_ref[...]).astype(o_ref.dtype)


def _in_proj(x2, gain, w, tm, tn):
    m = x2.shape[0]
    n = w.shape[1]
    return pl.pallas_call(
        _in_proj_body,
        grid=(m // tm, n // tn),
        in_specs=[
            pl.BlockSpec((tm, D_MODEL), lambda i, j: (i, 0)),
            pl.BlockSpec((1, D_MODEL), lambda i, j: (0, 0)),
            pl.BlockSpec((D_MODEL, tn), lambda i, j: (0, j)),
        ],
        out_specs=pl.BlockSpec((tm, tn), lambda i, j: (i, j)),
        out_shape=jax.ShapeDtypeStruct((m, n), BF16),
        scratch_shapes=[pltpu.VMEM((tm, D_MODEL), BF16)],
        compiler_params=_cparams(("parallel", "arbitrary")),
        name="in_proj",
    )(x2, gain, w)


def _qkv_up_body(cq_ref, ckv_ref, kra_ref, krb_ref, gq_ref, gkv_ref, wqt_ref, wk_ref, wvt_ref,
                 cq2_ref, sq2_ref, ck_ref, sk_ref, qt_ref, k_ref, vt_ref, qn2_ref, kn2_ref, *, qscale):
    ts = cq_ref.shape[0]
    cqn = _rms(cq_ref[...].astype(F32), gq_ref[...]).astype(BF16)
    ckvn = _rms(ckv_ref[...].astype(F32), gkv_ref[...]).astype(BF16)
    cq2 = cq2_ref[...]
    sq2 = sq2_ref[...]
    zeros_q = jnp.zeros((MLA_QK_PAD - MLA_NOPE - MLA_ROPE, ts), BF16)
    ones_v = jnp.ones((MLA_ACC_ROWS - MLA_V, ts), BF16)
    ones_n = jnp.ones((8, 128), BF16)
    for h in range(MLA_HEADS):
        qf = _dot_nt(wqt_ref[h], cqn) * qscale
        rope = qf[MLA_NOPE:MLA_NOPE + MLA_ROPE] * cq2 + qf[MLA_NOPE + MLA_ROPE:] * sq2
        qt_ref[0, h, 0:MLA_NOPE, :] = qf[0:MLA_NOPE].astype(BF16)
        qt_ref[0, h, MLA_NOPE:MLA_NOPE + MLA_ROPE, :] = rope.astype(BF16)
        qt_ref[0, h, MLA_NOPE + MLA_ROPE:, :] = zeros_q
        qn2 = (jnp.sum(qf[0:MLA_NOPE] * qf[0:MLA_NOPE], axis=0, keepdims=True)
               + jnp.sum(rope * rope, axis=0, keepdims=True))
        qn2_ref[0, h] = jnp.broadcast_to(qn2, (8, ts))
    kr = kra_ref[...].astype(F32) * ck_ref[...] + krb_ref[...].astype(F32) * sk_ref[...]
    lane = lax.broadcasted_iota(jnp.int32, kr.shape, 1)
    kr_aug = jnp.where(lane == MLA_ROPE, 1.0, kr).astype(BF16)
    kall = _dot(ckvn, wk_ref[...])
    for h in range(MLA_HEADS):
        kh = kall[:, h * MLA_NOPE:(h + 1) * MLA_NOPE]
        k_ref[0, h, :, 0:MLA_NOPE] = kh.astype(BF16)
        k_ref[0, h, :, MLA_NOPE:] = kr_aug
        kn2_ref[0, h] = _dot_nt(ones_n, (kh * kh + kr * kr).astype(BF16))
        vt_ref[0, h, 0:MLA_V, :] = _dot_nt(wvt_ref[h], ckvn).astype(BF16)
        vt_ref[0, h, MLA_V:, :] = ones_v


def _qkv_up(p, gq, gkv, wqt, wk, wvt, cq2, sq2, ck, sk, b, s, ts):
    ns = s // ts
    row = lambda bi, i: bi * ns + i
    qscale = float((MLA_NOPE + MLA_ROPE) ** -0.5 * LOG2E)
    full = lambda shape: pl.BlockSpec(shape, lambda bi, i: (0,) * len(shape))
    return pl.pallas_call(
        functools.partial(_qkv_up_body, qscale=qscale),
        grid=(b, ns),
        in_specs=[
            pl.BlockSpec((ts, MLA_Q_RANK), lambda bi, i: (row(bi, i), COL_CQ // MLA_Q_RANK)),
            pl.BlockSpec((ts, MLA_KV_RANK), lambda bi, i: (row(bi, i), COL_CKV // MLA_KV_RANK)),
            pl.BlockSpec((ts, 128), lambda bi, i: (row(bi, i), COL_KRA // 128)),
            pl.BlockSpec((ts, 128), lambda bi, i: (row(bi, i), COL_KRB // 128)),
            full((1, MLA_Q_RANK)),
            full((1, MLA_KV_RANK)),
            full((MLA_HEADS, MLA_QK_PAD, MLA_Q_RANK)),
            full((MLA_KV_RANK, MLA_HEADS * MLA_NOPE)),
            full((MLA_HEADS, MLA_V, MLA_KV_RANK)),
            pl.BlockSpec((MLA_ROPE, ts), lambda bi, i: (0, i)),
            pl.BlockSpec((MLA_ROPE, ts), lambda bi, i: (0, i)),
            pl.BlockSpec((ts, 128), lambda bi, i: (i, 0)),
            pl.BlockSpec((ts, 128), lambda bi, i: (i, 0)),
        ],
        out_specs=[
            pl.BlockSpec((1, MLA_HEADS, MLA_QK_PAD, ts), lambda bi, i: (bi, 0, 0, i)),
            pl.BlockSpec((1, MLA_HEADS, ts, MLA_QK_PAD), lambda bi, i: (bi, 0, i, 0)),
            pl.BlockSpec((1, MLA_HEADS, MLA_ACC_ROWS, ts), lambda bi, i: (bi, 0, 0, i)),
            pl.BlockSpec((1, MLA_HEADS, 8, ts), lambda bi, i: (bi, 0, 0, i)),
            pl.BlockSpec((1, MLA_HEADS, 8, ts), lambda bi, i: (bi, 0, 0, i)),
        ],
        out_shape=[
            jax.ShapeDtypeStruct((b, MLA_HEADS, MLA_QK_PAD, s), BF16),
            jax.ShapeDtypeStruct((b, MLA_HEADS, s, MLA_QK_PAD), BF16),
            jax.ShapeDtypeStruct((b, MLA_HEADS, MLA_ACC_ROWS, s), BF16),
            jax.ShapeDtypeStruct((b, MLA_HEADS, 8, s), F32),
            jax.ShapeDtypeStruct((b, MLA_HEADS, 8, s), F32),
        ],
        compiler_params=_cparams(("parallel", "parallel")),
        name="qkv_up",
    )(p, p, p, p, gq, gkv, wqt, wk, wvt, cq2, sq2, ck, sk)


MLA_GW = 256
MLA_QK_ROWS = 512
MLA_ACC_ROWS = 144
MLA_SHIFT_ROW = MLA_NOPE + MLA_ROPE
MLA_UB_MARGIN = 1.03125
MLA_L_MIN = 2.0 ** -100


def _mla_body(qt_ref, k_ref, vt_ref, qn2_ref, kn2_ref, o_ref, acc_ref, qs_ref, m_ref, p0, p1, kmax_ref, *, t):
    i = pl.program_id(2)
    n_groups = t // MLA_GW
    p_s = (p0, p1)

    def gcols(g):
        return slice(g * MLA_GW, (g + 1) * MLA_GW)

    @pl.when(i == 0)
    def _():
        kmax2 = jnp.max(jnp.max(kn2_ref[0, 0], axis=1, keepdims=True), axis=0, keepdims=True)
        kmax_ref[...] = jnp.broadcast_to(kmax2, kmax_ref.shape)

    ub = jnp.sqrt(qn2_ref[0, 0, 0:1, :] * kmax_ref[:, 0:1]) * MLA_UB_MARGIN
    qs_ref[...] = qt_ref[0, 0]
    row = lax.broadcasted_iota(jnp.int32, (16, t), 0)
    qs_ref[MLA_SHIFT_ROW:MLA_SHIFT_ROW + 16, :] = jnp.where(row == 0, -ub, 0.0).astype(BF16)
    acc_ref[...] = jnp.zeros_like(acc_ref)

    def probs(k0, rows, g, masked):
        ps = []
        for r in range(0, rows, MLA_QK_ROWS):
            n = min(MLA_QK_ROWS, rows - r)
            st = _dot(k_ref[0, 0, pl.ds(k0 + r, n), :], qs_ref[:, gcols(g)])
            if masked and r + n > rows - MLA_GW:
                kpos = r + lax.broadcasted_iota(jnp.int32, (n, MLA_GW), 0)
                qpos = rows - MLA_GW + lax.broadcasted_iota(jnp.int32, (n, MLA_GW), 1)
                st = jnp.where(kpos <= qpos, st, NEG)
            ps.append(jnp.exp2(st).astype(BF16))
        return ps[0] if len(ps) == 1 else jnp.concatenate(ps, axis=0)

    def key_off(j):
        return pl.multiple_of(jnp.where(j == 0, i, j - 1) * t, t)

    def pv(j, slot, g):
        acc_ref[g] += _dot(vt_ref[0, 0, :, pl.ds(key_off(j), t)], p_s[slot][g])

    def step(k, slot):
        for g in range(n_groups):
            p_s[slot][g] = probs(key_off(k), t, g, False)
            pv(k - 1, 1 - slot, g)

    for g in range(n_groups):
        rows = (g + 1) * MLA_GW
        p0[g, 0:rows, :] = probs(pl.multiple_of(i * t, t), rows, g, True)
        if rows < t:
            p0[g, rows:, :] = jnp.zeros((t - rows, MLA_GW), BF16)

    def quad(u, carry):
        for d in range(4):
            step(4 * u + d + 1, (d + 1) % 2)
        return carry

    lax.fori_loop(0, i // 4, quad, 0)

    @pl.when((i // 2) % 2 == 1)
    def _():
        k_base = (i // 4) * 4
        step(k_base + 1, 1)
        step(k_base + 2, 0)

    @pl.when(i % 2 == 1)
    def _():
        step(i, 1)
        for g in range(n_groups):
            pv(i, 1, g)

    @pl.when(i % 2 == 0)
    def _():
        for g in range(n_groups):
            pv(i, 0, g)

    l_min = acc_ref[0, MLA_V:MLA_V + 1, :]
    for g in range(1, n_groups):
        l_min = jnp.minimum(l_min, acc_ref[g, MLA_V:MLA_V + 1, :])
    l_min = jnp.min(l_min, axis=1, keepdims=True)

    @pl.when(jnp.logical_not(l_min[0, 0] >= MLA_L_MIN))
    def _():
        acc_ref[...] = jnp.zeros_like(acc_ref)
        m_ref[...] = jnp.full_like(m_ref, NEG)

        def exact_chunk(c, carry):
            k0 = pl.multiple_of(c * MLA_QK_ROWS, MLA_QK_ROWS)
            kc = k_ref[0, 0, pl.ds(k0, MLA_QK_ROWS), :]
            vt1 = vt_ref[0, 0, :, pl.ds(k0, MLA_QK_ROWS)]
            for g in range(n_groups):
                st = _dot(kc, qt_ref[0, 0, :, gcols(g)])
                kpos = k0 + lax.broadcasted_iota(jnp.int32, st.shape, 0)
                qpos = i * t + g * MLA_GW + lax.broadcasted_iota(jnp.int32, st.shape, 1)
                st = jnp.where(kpos <= qpos, st, NEG)
                m_old = m_ref[g]
                m_new = jnp.maximum(m_old, jnp.max(st, axis=0, keepdims=True))
                m_ref[g] = m_new
                p = jnp.exp2(st - m_new).astype(BF16)
                acc_ref[g] = jnp.exp2(m_old - m_new) * acc_ref[g] + _dot(vt1, p)
            return carry

        lax.fori_loop(0, (i + 1) * (t // MLA_QK_ROWS), exact_chunk, 0)

    for g in range(n_groups):
        out_t = acc_ref[g, 0:MLA_V, :] / acc_ref[g, MLA_V:MLA_V + 1, :]
        o_ref[0, g * MLA_GW:(g + 1) * MLA_GW, :] = out_t.T.astype(o_ref.dtype)


def _mla_attention(qt, k, vt, qn2, kn2, t):
    b, h, _, s = qt.shape
    n_groups = t // MLA_GW
    return pl.pallas_call(
        functools.partial(_mla_body, t=t),
        grid=(b, h, s // t),
        in_specs=[
            pl.BlockSpec((1, 1, MLA_QK_PAD, t), lambda bi, hi, i: (bi, hi, 0, i)),
            pl.BlockSpec((1, 1, s, MLA_QK_PAD), lambda bi, hi, i: (bi, hi, 0, 0)),
            pl.BlockSpec((1, 1, MLA_ACC_ROWS, s), lambda bi, hi, i: (bi, hi, 0, 0)),
            pl.BlockSpec((1, 1, 8, t), lambda bi, hi, i: (bi, hi, 0, i)),
            pl.BlockSpec((1, 1, 8, s), lambda bi, hi, i: (bi, hi, 0, 0)),
        ],
        out_specs=pl.BlockSpec((1, t, MLA_V), lambda bi, hi, i: (bi, i, hi)),
        out_shape=jax.ShapeDtypeStruct((b, s, h * MLA_V), BF16),
        scratch_shapes=[
            pltpu.VMEM((n_groups, MLA_ACC_ROWS, MLA_GW), F32),
            pltpu.VMEM((MLA_QK_PAD, t), BF16),
            pltpu.VMEM((n_groups, 1, MLA_GW), F32),
            pltpu.VMEM((n_groups, t, MLA_GW), BF16),
            pltpu.VMEM((n_groups, t, MLA_GW), BF16),
            pltpu.VMEM((1, 128), F32),
        ],
        compiler_params=_cparams(("parallel", "parallel", "arbitrary")),
        name="mla_attn",
    )(qt, k, vt, qn2, kn2)


def _t5_bucket_table():
    n = np.arange(REL_MAX_DIST)
    max_exact = REL_BUCKETS // 2
    large = max_exact + (np.log(np.maximum(n, 1).astype(np.float32) / max_exact)
                         / math.log(REL_MAX_DIST / max_exact) * (REL_BUCKETS - max_exact)).astype(np.int32)
    large = np.minimum(large, REL_BUCKETS - 1)
    return np.where(n < max_exact, n, large)


def _swa_bias_body(bucket_ref, table_ref, o_ref):
    bucket = bucket_ref[...]
    kj = lax.broadcasted_iota(jnp.int32, bucket.shape, 1)
    for h in range(SWA_Q_HEADS):
        bias = jnp.zeros(bucket.shape, F32)
        for bk in range(REL_BUCKETS):
            bias = jnp.where(bucket == bk, table_ref[bk, h], bias)
        valid = bucket >= 0
        bias = bias * LOG2E
        o_ref[1, h] = jnp.where(valid, bias, NEG)
        o_ref[0, h] = jnp.where(valid & (kj >= SWA_BLOCK), bias, NEG)


def _swa_bias(rel_table):
    qi = np.arange(SWA_BLOCK)[:, None]
    kj = np.arange(2 * SWA_BLOCK)[None, :]
    dist = qi + SWA_BLOCK - kj
    in_window = (dist >= 0) & (dist < SWA_BLOCK)
    bucket = np.where(in_window, _t5_bucket_table()[np.clip(dist, 0, REL_MAX_DIST - 1)], -1).astype(np.int32)
    return pl.pallas_call(
        _swa_bias_body,
        in_specs=[
            pl.BlockSpec(memory_space=pltpu.VMEM),
            pl.BlockSpec(memory_space=pltpu.SMEM),
        ],
        out_specs=pl.BlockSpec(memory_space=pltpu.VMEM),
        out_shape=jax.ShapeDtypeStruct((2, SWA_Q_HEADS, SWA_BLOCK, 2 * SWA_BLOCK), F32),
        name="swa_bias",
    )(jnp.asarray(bucket), rel_table)


def _swa_body(q_ref, kc_ref, kp_ref, vc_ref, vp_ref, bias_ref, sink_ref, o_ref):
    blk = SWA_BLOCK
    lane = lax.broadcasted_iota(jnp.int32, (2 * blk, 128), 1)
    lane_q = lax.broadcasted_iota(jnp.int32, (blk, 128), 1)
    kband = jnp.concatenate([kp_ref[...], kc_ref[...]], axis=0)
    vband = jnp.concatenate([vp_ref[...], vc_ref[...]], axis=0)

    def dup_half(x2, half):
        rolled = pltpu.roll(x2, 64, axis=1)
        first = lane < 64
        return jnp.where(first, x2, rolled) if half == 0 else jnp.where(first, rolled, x2)

    for kvh in range(SWA_KV_HEADS):
        pair_cols = slice((kvh // 2) * 128, (kvh // 2 + 1) * 128)
        k2 = dup_half(kband[:, pair_cols].astype(F32), kvh % 2).astype(BF16)
        v2 = dup_half(vband[:, pair_cols].astype(F32), kvh % 2).astype(BF16)
        parts = []
        for j in range(SWA_GROUP // 2):
            c0 = kvh * SWA_GROUP * SWA_DH + j * 128
            q2 = q_ref[:, c0:c0 + 128]
            zero = jnp.zeros_like(q2)
            parts.append(jnp.where(lane_q < 64, q2, zero))
            parts.append(jnp.where(lane_q < 64, zero, q2))
        qstack = jnp.concatenate(parts, axis=0)
        s_all = _dot_nt(qstack, k2)
        ps = []
        dens = []
        for g in range(SWA_GROUP):
            hq = kvh * SWA_GROUP + g
            sg = s_all[g * blk:(g + 1) * blk] + bias_ref[0, hq]
            sink = sink_ref[hq] * LOG2E
            m = jnp.maximum(jnp.max(sg, axis=-1, keepdims=True), sink)
            p = jnp.exp2(sg - m)
            dens.append(jnp.sum(p, axis=-1, keepdims=True) + jnp.exp2(sink - m))
            ps.append(p.astype(BF16))
        o_all = _dot(jnp.concatenate(ps, axis=0), v2)
        for j in range(SWA_GROUP // 2):
            oa = o_all[(2 * j) * blk:(2 * j + 1) * blk] / dens[2 * j]
            ob = o_all[(2 * j + 1) * blk:(2 * j + 2) * blk] / dens[2 * j + 1]
            c0 = kvh * SWA_GROUP * SWA_DH + j * 128
            o_ref[:, c0:c0 + 128] = jnp.where(lane_q < 64, oa, ob).astype(o_ref.dtype)


def _swa_attention(p, bias, sinks, b, s):
    nb = s // SWA_BLOCK
    kvw = SWA_KV_HEADS * SWA_DH
    row = lambda bi, n: bi * nb + n
    prow = lambda bi, n: bi * nb + jnp.maximum(n - 1, 0)
    return pl.pallas_call(
        _swa_body,
        grid=(b, nb),
        in_specs=[
            pl.BlockSpec((SWA_BLOCK, D_MODEL), lambda bi, n: (row(bi, n), COL_QS // D_MODEL)),
            pl.BlockSpec((SWA_BLOCK, kvw), lambda bi, n: (row(bi, n), COL_KS // kvw)),
            pl.BlockSpec((SWA_BLOCK, kvw), lambda bi, n: (prow(bi, n), COL_KS // kvw)),
            pl.BlockSpec((SWA_BLOCK, kvw), lambda bi, n: (row(bi, n), COL_VS // kvw)),
            pl.BlockSpec((SWA_BLOCK, kvw), lambda bi, n: (prow(bi, n), COL_VS // kvw)),
            pl.BlockSpec((1, SWA_Q_HEADS, SWA_BLOCK, 2 * SWA_BLOCK),
                         lambda bi, n: (jnp.minimum(n, 1), 0, 0, 0)),
            pl.BlockSpec(memory_space=pltpu.SMEM),
        ],
        out_specs=pl.BlockSpec((SWA_BLOCK, D_MODEL), lambda bi, n: (row(bi, n), 0)),
        out_shape=jax.ShapeDtypeStruct((b * s, D_MODEL), BF16),
        compiler_params=_cparams(("parallel", "arbitrary")),
        name="swa_attn",
    )(p, p, p, p, p, bias, sinks)


def _merge_body(oa_ref, ob_ref, wa_ref, wb_ref, g0_ref, g1_ref, o_ref):
    ya = _dot(oa_ref[...], wa_ref[...])
    yb = _dot(ob_ref[...], wb_ref[...])
    g0 = jax.nn.sigmoid(g0_ref[...].astype(F32))
    g1 = jax.nn.sigmoid(g1_ref[...].astype(F32))
    o_ref[...] = (g0 * ya + g1 * yb).astype(o_ref.dtype)


def _merge(oa, ob, wa, wb, p, tm, tn):
    m = oa.shape[0]
    nj = D_MODEL // tn
    return pl.pallas_call(
        _merge_body,
        grid=(m // tm, nj),
        in_specs=[
            pl.BlockSpec((tm, D_MODEL), lambda i, j: (i, 0)),
            pl.BlockSpec((tm, D_MODEL), lambda i, j: (i, 0)),
            pl.BlockSpec((D_MODEL, tn), lambda i, j: (0, j)),
            pl.BlockSpec((D_MODEL, tn), lambda i, j: (0, j)),
            pl.BlockSpec((tm, tn), lambda i, j: (i, COL_GATES // tn + j)),
            pl.BlockSpec((tm, tn), lambda i, j: (i, COL_GATES // tn + nj + j)),
        ],
        out_specs=pl.BlockSpec((tm, tn), lambda i, j: (i, j)),
        out_shape=jax.ShapeDtypeStruct((m, D_MODEL), BF16),
        compiler_params=_cparams(("parallel", "arbitrary")),
        name="gated_merge",
    )(oa, ob, wa, wb, p, p)


def _out_proj_body(mg_ref, w_ref, x_ref, g_ref, o_ref):
    y = _dot(mg_ref[...], w_ref[...])
    o_ref[...] = x_ref[...] + _rms(y, g_ref[...])


def _out_proj(mg, w, x2, gain, tm):
    m = mg.shape[0]
    return pl.pallas_call(
        _out_proj_body,
        grid=(m // tm,),
        in_specs=[
            pl.BlockSpec((tm, D_MODEL), lambda i: (i, 0)),
            pl.BlockSpec((D_MODEL, D_MODEL), lambda i: (0, 0)),
            pl.BlockSpec((tm, D_MODEL), lambda i: (i, 0)),
            pl.BlockSpec((1, D_MODEL), lambda i: (0, 0)),
        ],
        out_specs=pl.BlockSpec((tm, D_MODEL), lambda i: (i, 0)),
        out_shape=jax.ShapeDtypeStruct((m, D_MODEL), F32),
        compiler_params=_cparams(("parallel",)),
        name="out_proj",
    )(mg, w, x2, gain)


HALO = 16


def _ffn_body(x_ref, xh_ref, gpre_ref, wa_ref, wb_ref, cw_ref, cb_ref, wd_ref, gpost_ref, o_ref,
              hn_ref, a_ref, acc_ref, *, tiles_per_seq):
    i = pl.program_id(0)
    j = pl.program_id(1)
    tm = x_ref.shape[0]

    @pl.when(j == 0)
    def _():
        halo = _rms(xh_ref[...], gpre_ref[...])
        halo = jnp.where(i % tiles_per_seq == 0, jnp.zeros_like(halo), halo)
        hn_ref[0:HALO, :] = halo.astype(BF16)
        hn_ref[HALO:, :] = _rms(x_ref[...], gpre_ref[...]).astype(BF16)
        acc_ref[...] = jnp.zeros_like(acc_ref)

    a_ref[...] = _dot(hn_ref[...], wa_ref[...])
    gate = _dot(hn_ref[HALO:, :], wb_ref[...])
    c = cb_ref[...] + cw_ref[0:1, :] * a_ref[pl.ds(HALO - 2, tm), :]
    c = c + cw_ref[1:2, :] * a_ref[pl.ds(HALO - 1, tm), :]
    c = c + cw_ref[2:3, :] * a_ref[pl.ds(HALO, tm), :]
    act = (jax.nn.gelu(c, approximate=True) * gate).astype(BF16)
    acc_ref[...] += _dot(act, wd_ref[...])

    @pl.when(j == pl.num_programs(1) - 1)
    def _():
        o_ref[...] = x_ref[...] + _rms(acc_ref[...], gpost_ref[...])


def _ffn(x1, gain_pre, w_up, conv_w, conv_b, w_down, gain_post, s, tm, tn):
    m = x1.shape[0]
    nj = D_FF // tn
    hb = tm // HALO
    return pl.pallas_call(
        functools.partial(_ffn_body, tiles_per_seq=s // tm),
        grid=(m // tm, nj),
        in_specs=[
            pl.BlockSpec((tm, D_MODEL), lambda i, j: (i, 0)),
            pl.BlockSpec((HALO, D_MODEL), lambda i, j: (jnp.maximum(i * hb - 1, 0), 0)),
            pl.BlockSpec((1, D_MODEL), lambda i, j: (0, 0)),
            pl.BlockSpec((D_MODEL, tn), lambda i, j: (0, j)),
            pl.BlockSpec((D_MODEL, tn), lambda i, j: (0, nj + j)),
            pl.BlockSpec((3, tn), lambda i, j: (0, j)),
            pl.BlockSpec((1, tn), lambda i, j: (0, j)),
            pl.BlockSpec((tn, D_MODEL), lambda i, j: (j, 0)),
            pl.BlockSpec((1, D_MODEL), lambda i, j: (0, 0)),
        ],
        out_specs=pl.BlockSpec((tm, D_MODEL), lambda i, j: (i, 0)),
        out_shape=jax.ShapeDtypeStruct((m, D_MODEL), F32),
        scratch_shapes=[
            pltpu.VMEM((HALO + tm, D_MODEL), BF16),
            pltpu.VMEM((HALO + tm, tn), F32),
            pltpu.VMEM((tm, D_MODEL), F32),
        ],
        compiler_params=_cparams(("parallel", "arbitrary")),
        name="ffn_fused",
    )(x1, x1, gain_pre, w_up, w_up, conv_w, conv_b, w_down, gain_post)


def _prep_w_in(w_in):
    cq = w_in[:, 0:512]
    ckv = w_in[:, 512:768]
    kr = w_in[:, 768:832]
    qs = w_in[:, 832:2880] * SWA_QSCALE
    ks = w_in[:, 2880:3136]
    vs = w_in[:, 3136:3392]
    gates = w_in[:, 3392:7488]
    z64 = jnp.zeros((D_MODEL, 64), BF16)
    kr_sw = jnp.concatenate([kr[:, 32:], kr[:, :32]], axis=1)
    parts = [gates, qs, cq, ks, vs, ckv, kr, z64, kr_sw, z64]
    return jnp.concatenate([x.astype(BF16) for x in parts], axis=1)


def _prep_w_q_up(w):
    w = w.reshape(MLA_Q_RANK, MLA_HEADS, MLA_NOPE + MLA_ROPE)
    nope = w[:, :, :MLA_NOPE]
    rope = w[:, :, MLA_NOPE:]
    rope_sw = jnp.concatenate([rope[:, :, 32:], rope[:, :, :32]], axis=2)
    wq = jnp.concatenate([nope, rope, rope_sw], axis=2)
    return jnp.transpose(wq, (1, 2, 0)).astype(BF16)


def _prep_w_kv_up(w):
    w = w.reshape(MLA_KV_RANK, MLA_HEADS, MLA_NOPE + MLA_V)
    wk = w[:, :, :MLA_NOPE].reshape(MLA_KV_RANK, MLA_HEADS * MLA_NOPE).astype(BF16)
    wvt = jnp.transpose(w[:, :, MLA_NOPE:], (1, 2, 0)).astype(BF16)
    return wk, wvt


def _rope_tables(s):
    inv = ROPE_THETA ** (-jnp.arange(0, MLA_ROPE, 2, dtype=F32) / MLA_ROPE)
    ang = jnp.arange(s).astype(F32)[:, None] * inv[None, :]
    cos, sin = jnp.cos(ang), jnp.sin(ang)
    z = jnp.zeros((s, 64), F32)
    ck = jnp.concatenate([cos, cos, z], axis=1)
    sk = jnp.concatenate([-sin, sin, z], axis=1)
    cq2 = jnp.concatenate([cos, cos], axis=1).T
    sq2 = jnp.concatenate([-sin, sin], axis=1).T
    return cq2, sq2, ck, sk


def _tile(n, pref):
    t = min(n, pref)
    assert n % t == 0, (n, pref)
    return t


def kernel(x, norm_mix_pre, norm_mix_post, norm_ffn_pre, norm_ffn_post, w_in, mla_q_norm, mla_w_q_up, mla_kv_norm, mla_w_kv_up, swa_sinks, rel_bias_table, w_o_mla, w_o_swa, w_out, ffn_w_up, ffn_conv_w, ffn_conv_b, ffn_w_down):
    b, s, d = x.shape
    assert d == D_MODEL and s % SWA_BLOCK == 0
    depth = w_in.shape[0]
    m = b * s
    x2 = x.reshape(m, d)
    cq2, sq2, ck, sk = _rope_tables(s)
    bias = _swa_bias(rel_bias_table.astype(F32))
    for l in range(depth):
        p = _in_proj(x2, norm_mix_pre[l][None], _prep_w_in(w_in[l]), _tile(m, 1024), 1280)
        wk, wvt = _prep_w_kv_up(mla_w_kv_up[l])
        qt, k, vt, qn2, kn2 = _qkv_up(p, mla_q_norm[l][None], mla_kv_norm[l][None], _prep_w_q_up(mla_w_q_up[l]),
                            wk, wvt, cq2, sq2, ck, sk, b, s, _tile(s, 512))
        o_a = _mla_attention(qt, k, vt, qn2, kn2, _tile(s, 1024)).reshape(m, d)
        o_b = _swa_attention(p, bias, swa_sinks[l].astype(F32), b, s)
        mg = _merge(o_a, o_b, w_o_mla[l].astype(BF16), w_o_swa[l].astype(BF16), p, _tile(m, 1024), 1024)
        x1 = _out_proj(mg, w_out[l].astype(BF16), x2, norm_mix_post[l][None], _tile(m, 512))
        x2 = _ffn(x1, norm_ffn_pre[l][None], ffn_w_up[l].astype(BF16), ffn_conv_w[l], ffn_conv_b[l][None],
                  ffn_w_down[l].astype(BF16), norm_ffn_post[l][None], s, _tile(s, 512), 512)
    return x2.reshape(b, s, d)
```

```python
import functools
import math

import numpy as np
import jax
import jax.numpy as jnp
from jax import lax
from jax.experimental import pallas as pl
from jax.experimental.pallas import tpu as pltpu

F32 = jnp.float32
BF16 = jnp.bfloat16

D_MODEL = 2048
MLA_HEADS = 16
MLA_Q_RANK = 512
MLA_KV_RANK = 256
MLA_NOPE = 128
MLA_ROPE = 64
MLA_V = 128
MLA_QK_PAD = 256
ROPE_THETA = 10000.0
SWA_Q_HEADS = 32
SWA_KV_HEADS = 4
SWA_GROUP = 8
SWA_DH = 64
SWA_BLOCK = 128
REL_BUCKETS = 32
REL_MAX_DIST = 128
D_FF = 5632
EPS = 1e-6
NEG = -1e30
LOG2E = 1.4426950408889634
SWA_QSCALE = SWA_DH ** -0.5 * LOG2E

COL_GATES = 0
COL_QS = 4096
COL_CQ = 6144
COL_KS = 6656
COL_VS = 6912
COL_CKV = 7168
COL_KRA = 7424
COL_KRB = 7552
IN_COLS_PAD = 7680

VMEM_LIMIT = 56 * 1024 * 1024


def _cparams(sem, flags=None):
    return pltpu.CompilerParams(dimension_semantics=sem, vmem_limit_bytes=VMEM_LIMIT, flags=flags)


def _rms(x, g):
    return x * lax.rsqrt(jnp.mean(x * x, axis=-1, keepdims=True) + EPS) * g


def _dot(a, b):
    return jnp.dot(a, b, preferred_element_type=F32)


def _dot_nt(a, b):
    return lax.dot_general(a, b, (((1,), (1,)), ((), ())), preferred_element_type=F32)


def _in_proj_body(x_ref, g_ref, w_ref, o_ref, hn_ref):
    @pl.when(pl.program_id(1) == 0)
    def _():
        hn_ref[...] = _rms(x_ref[...], g_ref[...]).astype(BF16)

    o_ref[...] = _dot(hn_ref[...], w_ref[...]).astype(o_ref.dtype)


def _in_proj(x2, gain, w, tm, tn):
    m = x2.shape[0]
    n = w.shape[1]
    return pl.pallas_call(
        _in_proj_body,
        grid=(m // tm, n // tn),
        in_specs=[
            pl.BlockSpec((tm, D_MODEL), lambda i, j: (i, 0)),
            pl.BlockSpec((1, D_MODEL), lambda i, j: (0, 0)),
            pl.BlockSpec((D_MODEL, tn), lambda i, j: (0, j)),
        ],
        out_specs=pl.BlockSpec((tm, tn), lambda i, j: (i, j)),
        out_shape=jax.ShapeDtypeStruct((m, n), BF16),
        scratch_shapes=[pltpu.VMEM((tm, D_MODEL), BF16)],
        compiler_params=_cparams(("parallel", "arbitrary")),
        name="in_proj",
    )(x2, gain, w)


def _qkv_up_body(cq_ref, ckv_ref, kra_ref, krb_ref, gq_ref, gkv_ref, wqt_ref, wk_ref, wvt_ref,
                 cq2_ref, sq2_ref, ck_ref, sk_ref, qt_ref, k_ref, vt_ref, qn2_ref, kn2_ref, *, qscale):
    ts = cq_ref.shape[0]
    cqn = _rms(cq_ref[...].astype(F32), gq_ref[...]).astype(BF16)
    ckvn = _rms(ckv_ref[...].astype(F32), gkv_ref[...]).astype(BF16)
    cq2 = cq2_ref[...]
    sq2 = sq2_ref[...]
    zeros_q = jnp.zeros((MLA_QK_PAD - MLA_NOPE - MLA_ROPE, ts), BF16)
    ones_v = jnp.ones((MLA_ACC_ROWS - MLA_V, ts), BF16)
    ones_n = jnp.ones((8, 128), BF16)
    for h in range(MLA_HEADS):
        qf = _dot_nt(wqt_ref[h], cqn) * qscale
        rope = qf[MLA_NOPE:MLA_NOPE + MLA_ROPE] * cq2 + qf[MLA_NOPE + MLA_ROPE:] * sq2
        qt_ref[0, h, 0:MLA_NOPE, :] = qf[0:MLA_NOPE].astype(BF16)
        qt_ref[0, h, MLA_NOPE:MLA_NOPE + MLA_ROPE, :] = rope.astype(BF16)
        qt_ref[0, h, MLA_NOPE + MLA_ROPE:, :] = zeros_q
        qn2 = (jnp.sum(qf[0:MLA_NOPE] * qf[0:MLA_NOPE], axis=0, keepdims=True)
               + jnp.sum(rope * rope, axis=0, keepdims=True))
        qn2_ref[0, h] = jnp.broadcast_to(qn2, (8, ts))
    kr = kra_ref[...].astype(F32) * ck_ref[...] + krb_ref[...].astype(F32) * sk_ref[...]
    lane = lax.broadcasted_iota(jnp.int32, kr.shape, 1)
    kr_aug = jnp.where(lane == MLA_ROPE, 1.0, kr).astype(BF16)
    kall = _dot(ckvn, wk_ref[...])
    for h in range(MLA_HEADS):
        kh = kall[:, h * MLA_NOPE:(h + 1) * MLA_NOPE]
        k_ref[0, h, :, 0:MLA_NOPE] = kh.astype(BF16)
        k_ref[0, h, :, MLA_NOPE:] = kr_aug
        kn2_ref[0, h] = _dot_nt(ones_n, (kh * kh + kr * kr).astype(BF16))
        vt_ref[0, h, 0:MLA_V, :] = _dot_nt(wvt_ref[h], ckvn).astype(BF16)
        vt_ref[0, h, MLA_V:, :] = ones_v


def _qkv_up(p, gq, gkv, wqt, wk, wvt, cq2, sq2, ck, sk, b, s, ts):
    ns = s // ts
    row = lambda bi, i: bi * ns + i
    qscale = float((MLA_NOPE + MLA_ROPE) ** -0.5 * LOG2E)
    full = lambda shape: pl.BlockSpec(shape, lambda bi, i: (0,) * len(shape))
    return pl.pallas_call(
        functools.partial(_qkv_up_body, qscale=qscale),
        grid=(b, ns),
        in_specs=[
            pl.BlockSpec((ts, MLA_Q_RANK), lambda bi, i: (row(bi, i), COL_CQ // MLA_Q_RANK)),
            pl.BlockSpec((ts, MLA_KV_RANK), lambda bi, i: (row(bi, i), COL_CKV // MLA_KV_RANK)),
            pl.BlockSpec((ts, 128), lambda bi, i: (row(bi, i), COL_KRA // 128)),
            pl.BlockSpec((ts, 128), lambda bi, i: (row(bi, i), COL_KRB // 128)),
            full((1, MLA_Q_RANK)),
            full((1, MLA_KV_RANK)),
            full((MLA_HEADS, MLA_QK_PAD, MLA_Q_RANK)),
            full((MLA_KV_RANK, MLA_HEADS * MLA_NOPE)),
            full((MLA_HEADS, MLA_V, MLA_KV_RANK)),
            pl.BlockSpec((MLA_ROPE, ts), lambda bi, i: (0, i)),
            pl.BlockSpec((MLA_ROPE, ts), lambda bi, i: (0, i)),
            pl.BlockSpec((ts, 128), lambda bi, i: (i, 0)),
            pl.BlockSpec((ts, 128), lambda bi, i: (i, 0)),
        ],
        out_specs=[
            pl.BlockSpec((1, MLA_HEADS, MLA_QK_PAD, ts), lambda bi, i: (bi, 0, 0, i)),
            pl.BlockSpec((1, MLA_HEADS, ts, MLA_QK_PAD), lambda bi, i: (bi, 0, i, 0)),
            pl.BlockSpec((1, MLA_HEADS, MLA_ACC_ROWS, ts), lambda bi, i: (bi, 0, 0, i)),
            pl.BlockSpec((1, MLA_HEADS, 8, ts), lambda bi, i: (bi, 0, 0, i)),
            pl.BlockSpec((1, MLA_HEADS, 8, ts), lambda bi, i: (bi, 0, 0, i)),
        ],
        out_shape=[
            jax.ShapeDtypeStruct((b, MLA_HEADS, MLA_QK_PAD, s), BF16),
            jax.ShapeDtypeStruct((b, MLA_HEADS, s, MLA_QK_PAD), BF16),
            jax.ShapeDtypeStruct((b, MLA_HEADS, MLA_ACC_ROWS, s), BF16),
            jax.ShapeDtypeStruct((b, MLA_HEADS, 8, s), F32),
            jax.ShapeDtypeStruct((b, MLA_HEADS, 8, s), F32),
        ],
        compiler_params=_cparams(("parallel", "parallel")),
        name="qkv_up",
    )(p, p, p, p, gq, gkv, wqt, wk, wvt, cq2, sq2, ck, sk)


MLA_GW = 256
MLA_QK_ROWS = 512
MLA_ACC_ROWS = 144
MLA_SHIFT_ROW = MLA_NOPE + MLA_ROPE
MLA_UB_MARGIN = 1.03125
MLA_L_MIN = 2.0 ** -100


def _mla_body(qt_ref, k_ref, vt_ref, qn2_ref, kn2_ref, o_ref, acc_ref, qs_ref, m_ref, p0, p1, kmax_ref, *, t):
    i = pl.program_id(2)
    n_groups = t // MLA_GW
    p_s = (p0, p1)

    def gcols(g):
        return slice(g * MLA_GW, (g + 1) * MLA_GW)

    @pl.when(i == 0)
    def _():
        kmax2 = jnp.max(jnp.max(kn2_ref[0, 0], axis=1, keepdims=True), axis=0, keepdims=True)
        kmax_ref[...] = jnp.broadcast_to(kmax2, kmax_ref.shape)

    ub = jnp.sqrt(qn2_ref[0, 0, 0:1, :] * kmax_ref[:, 0:1]) * MLA_UB_MARGIN
    qs_ref[...] = qt_ref[0, 0]
    row = lax.broadcasted_iota(jnp.int32, (16, t), 0)
    qs_ref[MLA_SHIFT_ROW:MLA_SHIFT_ROW + 16, :] = jnp.where(row == 0, -ub, 0.0).astype(BF16)
    acc_ref[...] = jnp.zeros_like(acc_ref)

    def probs(k0, rows, g, masked):
        ps = []
        for r in range(0, rows, MLA_QK_ROWS):
            n = min(MLA_QK_ROWS, rows - r)
            st = _dot(k_ref[0, 0, pl.ds(k0 + r, n), :], qs_ref[:, gcols(g)])
            if masked and r + n > rows - MLA_GW:
                kpos = r + lax.broadcasted_iota(jnp.int32, (n, MLA_GW), 0)
                qpos = rows - MLA_GW + lax.broadcasted_iota(jnp.int32, (n, MLA_GW), 1)
                st = jnp.where(kpos <= qpos, st, NEG)
            ps.append(jnp.exp2(st).astype(BF16))
        return ps[0] if len(ps) == 1 else jnp.concatenate(ps, axis=0)

    def key_off(j):
        return pl.multiple_of(jnp.where(j == 0, i, j - 1) * t, t)

    def pv(j, slot, g):
        acc_ref[g] += _dot(vt_ref[0, 0, :, pl.ds(key_off(j), t)], p_s[slot][g])

    def step(k, slot):
        for g in range(n_groups):
            p_s[slot][g] = probs(key_off(k), t, g, False)
            pv(k - 1, 1 - slot, g)

    for g in range(n_groups):
        rows = (g + 1) * MLA_GW
        p0[g, 0:rows, :] = probs(pl.multiple_of(i * t, t), rows, g, True)
        if rows < t:
            p0[g, rows:, :] = jnp.zeros((t - rows, MLA_GW), BF16)

    def quad(u, carry):
        for d in range(4):
            step(4 * u + d + 1, (d + 1) % 2)
        return carry

    lax.fori_loop(0, i // 4, quad, 0)

    @pl.when((i // 2) % 2 == 1)
    def _():
        k_base = (i // 4) * 4
        step(k_base + 1, 1)
        step(k_base + 2, 0)

    @pl.when(i % 2 == 1)
    def _():
        step(i, 1)
        for g in range(n_groups):
            pv(i, 1, g)

    @pl.when(i % 2 == 0)
    def _():
        for g in range(n_groups):
            pv(i, 0, g)

    l_min = acc_ref[0, MLA_V:MLA_V + 1, :]
    for g in range(1, n_groups):
        l_min = jnp.minimum(l_min, acc_ref[g, MLA_V:MLA_V + 1, :])
    l_min = jnp.min(l_min, axis=1, keepdims=True)

    @pl.when(jnp.logical_not(l_min[0, 0] >= MLA_L_MIN))
    def _():
        acc_ref[...] = jnp.zeros_like(acc_ref)
        m_ref[...] = jnp.full_like(m_ref, NEG)

        def exact_chunk(c, carry):
            k0 = pl.multiple_of(c * MLA_QK_ROWS, MLA_QK_ROWS)
            kc = k_ref[0, 0, pl.ds(k0, MLA_QK_ROWS), :]
            vt1 = vt_ref[0, 0, :, pl.ds(k0, MLA_QK_ROWS)]
            for g in range(n_groups):
                st = _dot(kc, qt_ref[0, 0, :, gcols(g)])
                kpos = k0 + lax.broadcasted_iota(jnp.int32, st.shape, 0)
                qpos = i * t + g * MLA_GW + lax.broadcasted_iota(jnp.int32, st.shape, 1)
                st = jnp.where(kpos <= qpos, st, NEG)
                m_old = m_ref[g]
                m_new = jnp.maximum(m_old, jnp.max(st, axis=0, keepdims=True))
                m_ref[g] = m_new
                p = jnp.exp2(st - m_new).astype(BF16)
                acc_ref[g] = jnp.exp2(m_old - m_new) * acc_ref[g] + _dot(vt1, p)
            return carry

        lax.fori_loop(0, (i + 1) * (t // MLA_QK_ROWS), exact_chunk, 0)

    for g in range(n_groups):
        out_t = acc_ref[g, 0:MLA_V, :] / acc_ref[g, MLA_V:MLA_V + 1, :]
        o_ref[0, g * MLA_GW:(g + 1) * MLA_GW, :] = out_t.T.astype(o_ref.dtype)


def _mla_attention(qt, k, vt, qn2, kn2, t):
    b, h, _, s = qt.shape
    n_groups = t // MLA_GW
    return pl.pallas_call(
        functools.partial(_mla_body, t=t),
        grid=(b, h, s // t),
        in_specs=[
            pl.BlockSpec((1, 1, MLA_QK_PAD, t), lambda bi, hi, i: (bi, hi, 0, i)),
            pl.BlockSpec((1, 1, s, MLA_QK_PAD), lambda bi, hi, i: (bi, hi, 0, 0)),
            pl.BlockSpec((1, 1, MLA_ACC_ROWS, s), lambda bi, hi, i: (bi, hi, 0, 0)),
            pl.BlockSpec((1, 1, 8, t), lambda bi, hi, i: (bi, hi, 0, i)),
            pl.BlockSpec((1, 1, 8, s), lambda bi, hi, i: (bi, hi, 0, 0)),
        ],
        out_specs=pl.BlockSpec((1, t, MLA_V), lambda bi, hi, i: (bi, i, hi)),
        out_shape=jax.ShapeDtypeStruct((b, s, h * MLA_V), BF16),
        scratch_shapes=[
            pltpu.VMEM((n_groups, MLA_ACC_ROWS, MLA_GW), F32),
            pltpu.VMEM((MLA_QK_PAD, t), BF16),
            pltpu.VMEM((n_groups, 1, MLA_GW), F32),
            pltpu.VMEM((n_groups, t, MLA_GW), BF16),
            pltpu.VMEM((n_groups, t, MLA_GW), BF16),
            pltpu.VMEM((1, 128), F32),
        ],
        compiler_params=_cparams(("parallel", "parallel", "arbitrary")),
        name="mla_attn",
    )(qt, k, vt, qn2, kn2)


def _t5_bucket_table():
    n = np.arange(REL_MAX_DIST)
    max_exact = REL_BUCKETS // 2
    large = max_exact + (np.log(np.maximum(n, 1).astype(np.float32) / max_exact)
                         / math.log(REL_MAX_DIST / max_exact) * (REL_BUCKETS - max_exact)).astype(np.int32)
    large = np.minimum(large, REL_BUCKETS - 1)
    return np.where(n < max_exact, n, large)


def _swa_bias_body(bucket_ref, table_ref, o_ref):
    bucket = bucket_ref[...]
    kj = lax.broadcasted_iota(jnp.int32, bucket.shape, 1)
    for h in range(SWA_Q_HEADS):
        bias = jnp.zeros(bucket.shape, F32)
        for bk in range(REL_BUCKETS):
            bias = jnp.where(bucket == bk, table_ref[bk, h], bias)
        valid = bucket >= 0
        bias = bias * LOG2E
        o_ref[1, h] = jnp.where(valid, bias, NEG)
        o_ref[0, h] = jnp.where(valid & (kj >= SWA_BLOCK), bias, NEG)


def _swa_bias(rel_table):
    qi = np.arange(SWA_BLOCK)[:, None]
    kj = np.arange(2 * SWA_BLOCK)[None, :]
    dist = qi + SWA_BLOCK - kj
    in_window = (dist >= 0) & (dist < SWA_BLOCK)
    bucket = np.where(in_window, _t5_bucket_table()[np.clip(dist, 0, REL_MAX_DIST - 1)], -1).astype(np.int32)
    return pl.pallas_call(
        _swa_bias_body,
        in_specs=[
            pl.BlockSpec(memory_space=pltpu.VMEM),
            pl.BlockSpec(memory_space=pltpu.SMEM),
        ],
        out_specs=pl.BlockSpec(memory_space=pltpu.VMEM),
        out_shape=jax.ShapeDtypeStruct((2, SWA_Q_HEADS, SWA_BLOCK, 2 * SWA_BLOCK), F32),
        name="swa_bias",
    )(jnp.asarray(bucket), rel_table)


def _swa_body(q_ref, kc_ref, kp_ref, vc_ref, vp_ref, bias_ref, sink_ref, o_ref):
    blk = SWA_BLOCK
    lane = lax.broadcasted_iota(jnp.int32, (2 * blk, 128), 1)
    lane_q = lax.broadcasted_iota(jnp.int32, (blk, 128), 1)
    kband = jnp.concatenate([kp_ref[...], kc_ref[...]], axis=0)
    vband = jnp.concatenate([vp_ref[...], vc_ref[...]], axis=0)

    def dup_half(x2, half):
        rolled = pltpu.roll(x2, 64, axis=1)
        first = lane < 64
        return jnp.where(first, x2, rolled) if half == 0 else jnp.where(first, rolled, x2)

    for kvh in range(SWA_KV_HEADS):
        pair_cols = slice((kvh // 2) * 128, (kvh // 2 + 1) * 128)
        k2 = dup_half(kband[:, pair_cols].astype(F32), kvh % 2).astype(BF16)
        v2 = dup_half(vband[:, pair_cols].astype(F32), kvh % 2).astype(BF16)
        parts = []
        for j in range(SWA_GROUP // 2):
            c0 = kvh * SWA_GROUP * SWA_DH + j * 128
            q2 = q_ref[:, c0:c0 + 128]
            zero = jnp.zeros_like(q2)
            parts.append(jnp.where(lane_q < 64, q2, zero))
            parts.append(jnp.where(lane_q < 64, zero, q2))
        qstack = jnp.concatenate(parts, axis=0)
        s_all = _dot_nt(qstack, k2)
        ps = []
        dens = []
        for g in range(SWA_GROUP):
            hq = kvh * SWA_GROUP + g
            sg = s_all[g * blk:(g + 1) * blk] + bias_ref[0, hq]
            sink = sink_ref[hq] * LOG2E
            m = jnp.maximum(jnp.max(sg, axis=-1, keepdims=True), sink)
            p = jnp.exp2(sg - m)
            dens.append(jnp.sum(p, axis=-1, keepdims=True) + jnp.exp2(sink - m))
            ps.append(p.astype(BF16))
        o_all = _dot(jnp.concatenate(ps, axis=0), v2)
        for j in range(SWA_GROUP // 2):
            oa = o_all[(2 * j) * blk:(2 * j + 1) * blk] / dens[2 * j]
            ob = o_all[(2 * j + 1) * blk:(2 * j + 2) * blk] / dens[2 * j + 1]
            c0 = kvh * SWA_GROUP * SWA_DH + j * 128
            o_ref[:, c0:c0 + 128] = jnp.where(lane_q < 64, oa, ob).astype(o_ref.dtype)


def _swa_attention(p, bias, sinks, b, s):
    nb = s // SWA_BLOCK
    kvw = SWA_KV_HEADS * SWA_DH
    row = lambda bi, n: bi * nb + n
    prow = lambda bi, n: bi * nb + jnp.maximum(n - 1, 0)
    return pl.pallas_call(
        _swa_body,
        grid=(b, nb),
        in_specs=[
            pl.BlockSpec((SWA_BLOCK, D_MODEL), lambda bi, n: (row(bi, n), COL_QS // D_MODEL)),
            pl.BlockSpec((SWA_BLOCK, kvw), lambda bi, n: (row(bi, n), COL_KS // kvw)),
            pl.BlockSpec((SWA_BLOCK, kvw), lambda bi, n: (prow(bi, n), COL_KS // kvw)),
            pl.BlockSpec((SWA_BLOCK, kvw), lambda bi, n: (row(bi, n), COL_VS // kvw)),
            pl.BlockSpec((SWA_BLOCK, kvw), lambda bi, n: (prow(bi, n), COL_VS // kvw)),
            pl.BlockSpec((1, SWA_Q_HEADS, SWA_BLOCK, 2 * SWA_BLOCK),
                         lambda bi, n: (jnp.minimum(n, 1), 0, 0, 0)),
            pl.BlockSpec(memory_space=pltpu.SMEM),
        ],
        out_specs=pl.BlockSpec((SWA_BLOCK, D_MODEL), lambda bi, n: (row(bi, n), 0)),
        out_shape=jax.ShapeDtypeStruct((b * s, D_MODEL), BF16),
        compiler_params=_cparams(("parallel", "arbitrary")),
        name="swa_attn",
    )(p, p, p, p, p, bias, sinks)


def _merge_body(oa_ref, ob_ref, wa_ref, wb_ref, g0_ref, g1_ref, o_ref):
    ya = _dot(oa_ref[...], wa_ref[...])
    yb = _dot(ob_ref[...], wb_ref[...])
    g0 = jax.nn.sigmoid(g0_ref[...].astype(F32))
    g1 = jax.nn.sigmoid(g1_ref[...].astype(F32))
    o_ref[...] = (g0 * ya + g1 * yb).astype(o_ref.dtype)


def _merge(oa, ob, wa, wb, p, tm, tn):
    m = oa.shape[0]
    nj = D_MODEL // tn
    return pl.pallas_call(
        _merge_body,
        grid=(m // tm, nj),
        in_specs=[
            pl.BlockSpec((tm, D_MODEL), lambda i, j: (i, 0)),
            pl.BlockSpec((tm, D_MODEL), lambda i, j: (i, 0)),
            pl.BlockSpec((D_MODEL, tn), lambda i, j: (0, j)),
            pl.BlockSpec((D_MODEL, tn), lambda i, j: (0, j)),
            pl.BlockSpec((tm, tn), lambda i, j: (i, COL_GATES // tn + j)),
            pl.BlockSpec((tm, tn), lambda i, j: (i, COL_GATES // tn + nj + j)),
        ],
        out_specs=pl.BlockSpec((tm, tn), lambda i, j: (i, j)),
        out_shape=jax.ShapeDtypeStruct((m, D_MODEL), BF16),
        compiler_params=_cparams(("parallel", "arbitrary")),
        name="gated_merge",
    )(oa, ob, wa, wb, p, p)


def _out_proj_body(mg_ref, w_ref, x_ref, g_ref, o_ref):
    y = _dot(mg_ref[...], w_ref[...])
    o_ref[...] = x_ref[...] + _rms(y, g_ref[...])


def _out_proj(mg, w, x2, gain, tm):
    m = mg.shape[0]
    return pl.pallas_call(
        _out_proj_body,
        grid=(m // tm,),
        in_specs=[
            pl.BlockSpec((tm, D_MODEL), lambda i: (i, 0)),
            pl.BlockSpec((D_MODEL, D_MODEL), lambda i: (0, 0)),
            pl.BlockSpec((tm, D_MODEL), lambda i: (i, 0)),
            pl.BlockSpec((1, D_MODEL), lambda i: (0, 0)),
        ],
        out_specs=pl.BlockSpec((tm, D_MODEL), lambda i: (i, 0)),
        out_shape=jax.ShapeDtypeStruct((m, D_MODEL), F32),
        compiler_params=_cparams(("parallel",)),
        name="out_proj",
    )(mg, w, x2, gain)


HALO = 16


def _ffn_body(x_ref, xh_ref, gpre_ref, wa_ref, wb_ref, cw_ref, cb_ref, wd_ref, gpost_ref, o_ref,
              hn_ref, a_ref, acc_ref, act_ref, *, tiles_per_seq):
    i = pl.program_id(0)
    j = pl.program_id(1)
    nj = pl.num_programs(1) - 1
    tm = x_ref.shape[0]

    def up():
        a_ref[...] = _dot(hn_ref[...], wa_ref[...])
        gate = _dot(hn_ref[HALO:, :], wb_ref[...])
        c = cb_ref[...] + cw_ref[0:1, :] * a_ref[pl.ds(HALO - 2, tm), :]
        c = c + cw_ref[1:2, :] * a_ref[pl.ds(HALO - 1, tm), :]
        c = c + cw_ref[2:3, :] * a_ref[pl.ds(HALO, tm), :]
        act_ref[...] = (jax.nn.gelu(c, approximate=True) * gate).astype(BF16)

    def down():
        acc_ref[...] += _dot(act_ref[...], wd_ref[...])

    @pl.when(j == 0)
    def _():
        halo = _rms(xh_ref[...], gpre_ref[...])
        halo = jnp.where(i % tiles_per_seq == 0, jnp.zeros_like(halo), halo)
        hn_ref[0:HALO, :] = halo.astype(BF16)
        hn_ref[HALO:, :] = _rms(x_ref[...], gpre_ref[...]).astype(BF16)
        acc_ref[...] = jnp.zeros_like(acc_ref)
        up()

    @pl.when((j > 0) & (j < nj))
    def _():
        down()
        up()

    @pl.when(j == nj)
    def _():
        down()
        o_ref[...] = x_ref[...] + _rms(acc_ref[...], gpost_ref[...])


def _ffn(x1, gain_pre, w_up, conv_w, conv_b, w_down, gain_post, s, tm, tn):
    m = x1.shape[0]
    nj = D_FF // tn
    hb = tm // HALO
    up_col = lambda j: jnp.minimum(j, nj - 1)
    down_row = lambda j: jnp.maximum(j - 1, 0)
    return pl.pallas_call(
        functools.partial(_ffn_body, tiles_per_seq=s // tm),
        grid=(m // tm, nj + 1),
        in_specs=[
            pl.BlockSpec((tm, D_MODEL), lambda i, j: (i, 0)),
            pl.BlockSpec((HALO, D_MODEL), lambda i, j: (jnp.maximum(i * hb - 1, 0), 0)),
            pl.BlockSpec((1, D_MODEL), lambda i, j: (0, 0)),
            pl.BlockSpec((D_MODEL, tn), lambda i, j: (0, up_col(j))),
            pl.BlockSpec((D_MODEL, tn), lambda i, j: (0, nj + up_col(j))),
            pl.BlockSpec((3, tn), lambda i, j: (0, up_col(j))),
            pl.BlockSpec((1, tn), lambda i, j: (0, up_col(j))),
            pl.BlockSpec((tn, D_MODEL), lambda i, j: (down_row(j), 0)),
            pl.BlockSpec((1, D_MODEL), lambda i, j: (0, 0)),
        ],
        out_specs=pl.BlockSpec((tm, D_MODEL), lambda i, j: (i, 0)),
        out_shape=jax.ShapeDtypeStruct((m, D_MODEL), F32),
        scratch_shapes=[
            pltpu.VMEM((HALO + tm, D_MODEL), BF16),
            pltpu.VMEM((HALO + tm, tn), F32),
            pltpu.VMEM((tm, D_MODEL), F32),
            pltpu.VMEM((tm, tn), BF16),
        ],
        compiler_params=_cparams(("parallel", "arbitrary")),
        name="ffn_fused",
    )(x1, x1, gain_pre, w_up, w_up, conv_w, conv_b, w_down, gain_post)


def _prep_w_in(w_in):
    cq = w_in[:, 0:512]
    ckv = w_in[:, 512:768]
    kr = w_in[:, 768:832]
    qs = w_in[:, 832:2880] * SWA_QSCALE
    ks = w_in[:, 2880:3136]
    vs = w_in[:, 3136:3392]
    gates = w_in[:, 3392:7488]
    z64 = jnp.zeros((D_MODEL, 64), BF16)
    kr_sw = jnp.concatenate([kr[:, 32:], kr[:, :32]], axis=1)
    parts = [gates, qs, cq, ks, vs, ckv, kr, z64, kr_sw, z64]
    return jnp.concatenate([x.astype(BF16) for x in parts], axis=1)


def _prep_w_q_up(w):
    w = w.reshape(MLA_Q_RANK, MLA_HEADS, MLA_NOPE + MLA_ROPE)
    nope = w[:, :, :MLA_NOPE]
    rope = w[:, :, MLA_NOPE:]
    rope_sw = jnp.concatenate([rope[:, :, 32:], rope[:, :, :32]], axis=2)
    wq = jnp.concatenate([nope, rope, rope_sw], axis=2)
    return jnp.transpose(wq, (1, 2, 0)).astype(BF16)


def _prep_w_kv_up(w):
    w = w.reshape(MLA_KV_RANK, MLA_HEADS, MLA_NOPE + MLA_V)
    wk = w[:, :, :MLA_NOPE].reshape(MLA_KV_RANK, MLA_HEADS * MLA_NOPE).astype(BF16)
    wvt = jnp.transpose(w[:, :, MLA_NOPE:], (1, 2, 0)).astype(BF16)
    return wk, wvt


def _rope_tables(s):
    inv = ROPE_THETA ** (-jnp.arange(0, MLA_ROPE, 2, dtype=F32) / MLA_ROPE)
    ang = jnp.arange(s).astype(F32)[:, None] * inv[None, :]
    cos, sin = jnp.cos(ang), jnp.sin(ang)
    z = jnp.zeros((s, 64), F32)
    ck = jnp.concatenate([cos, cos, z], axis=1)
    sk = jnp.concatenate([-sin, sin, z], axis=1)
    cq2 = jnp.concatenate([cos, cos], axis=1).T
    sq2 = jnp.concatenate([-sin, sin], axis=1).T
    return cq2, sq2, ck, sk


def _tile(n, pref):
    t = min(n, pref)
    assert n % t == 0, (n, pref)
    return t


TILE_IN_PROJ = (1024, 1280)
TILE_QKV_ROWS = 512
TILE_MLA = 1024
TILE_MERGE = (1024, 1024)
TILE_OUT_ROWS = 512
TILE_FFN = (512, 512)


def kernel(x, norm_mix_pre, norm_mix_post, norm_ffn_pre, norm_ffn_post, w_in, mla_q_norm, mla_w_q_up, mla_kv_norm, mla_w_kv_up, swa_sinks, rel_bias_table, w_o_mla, w_o_swa, w_out, ffn_w_up, ffn_conv_w, ffn_conv_b, ffn_w_down):
    b, s, d = x.shape
    assert d == D_MODEL and s % SWA_BLOCK == 0
    depth = w_in.shape[0]
    m = b * s
    x2 = x.reshape(m, d)
    cq2, sq2, ck, sk = _rope_tables(s)
    bias = _swa_bias(rel_bias_table.astype(F32))
    for l in range(depth):
        p = _in_proj(x2, norm_mix_pre[l][None], _prep_w_in(w_in[l]), _tile(m, TILE_IN_PROJ[0]), TILE_IN_PROJ[1])
        wk, wvt = _prep_w_kv_up(mla_w_kv_up[l])
        qt, k, vt, qn2, kn2 = _qkv_up(p, mla_q_norm[l][None], mla_kv_norm[l][None], _prep_w_q_up(mla_w_q_up[l]),
                                      wk, wvt, cq2, sq2, ck, sk, b, s, _tile(s, TILE_QKV_ROWS))
        o_a = _mla_attention(qt, k, vt, qn2, kn2, _tile(s, TILE_MLA)).reshape(m, d)
        o_b = _swa_attention(p, bias, swa_sinks[l].astype(F32), b, s)
        mg = _merge(o_a, o_b, w_o_mla[l].astype(BF16), w_o_swa[l].astype(BF16), p,
                    _tile(m, TILE_MERGE[0]), TILE_MERGE[1])
        x1 = _out_proj(mg, w_out[l].astype(BF16), x2, norm_mix_post[l][None], _tile(m, TILE_OUT_ROWS))
        x2 = _ffn(x1, norm_ffn_pre[l][None], ffn_w_up[l].astype(BF16), ffn_conv_w[l], ffn_conv_b[l][None],
                  ffn_w_down[l].astype(BF16), norm_ffn_post[l][None], s, _tile(s, TILE_FFN[0]), TILE_FFN[1])
    return x2.reshape(b, s, d)
```

```python
import functools
import math

import numpy as np
import jax
import jax.numpy as jnp
from jax import lax
from jax.experimental import pallas as pl
from jax.experimental.pallas import tpu as pltpu

F32 = jnp.float32
BF16 = jnp.bfloat16

D_MODEL = 2048
MLA_HEADS = 16
MLA_Q_RANK = 512
MLA_KV_RANK = 256
MLA_NOPE = 128
MLA_ROPE = 64
MLA_V = 128
MLA_QK_PAD = 256
ROPE_THETA = 10000.0
SWA_Q_HEADS = 32
SWA_KV_HEADS = 4
SWA_GROUP = 8
SWA_DH = 64
SWA_BLOCK = 128
REL_BUCKETS = 32
REL_MAX_DIST = 128
D_FF = 5632
EPS = 1e-6
NEG = -1e30
LOG2E = 1.4426950408889634
SWA_QSCALE = SWA_DH ** -0.5 * LOG2E

COL_GATES = 0
COL_QS = 4096
COL_CQ = 6144
COL_KS = 6656
COL_VS = 6912
COL_CKV = 7168
COL_KRA = 7424
COL_KRB = 7552
IN_COLS_PAD = 7680

VMEM_LIMIT = 56 * 1024 * 1024


def _cparams(sem, flags=None):
    return pltpu.CompilerParams(dimension_semantics=sem, vmem_limit_bytes=VMEM_LIMIT, flags=flags)


def _rms(x, g):
    return x * lax.rsqrt(jnp.mean(x * x, axis=-1, keepdims=True) + EPS) * g


def _dot(a, b):
    return jnp.dot(a, b, preferred_element_type=F32)


def _dot_nt(a, b):
    return lax.dot_general(a, b, (((1,), (1,)), ((), ())), preferred_element_type=F32)


def _in_proj_body(x_ref, g_ref, w_ref, o_ref, hn_ref):
    @pl.when(pl.program_id(1) == 0)
    def _():
        hn_ref[...] = _rms(x_ref[...], g_ref[...]).astype(BF16)

    o_ref[...] = _dot(hn_ref[...], w_ref[...]).astype(o_ref.dtype)


def _in_proj(x2, gain, w, tm, tn):
    m = x2.shape[0]
    n = w.shape[1]
    return pl.pallas_call(
        _in_proj_body,
        grid=(m // tm, n // tn),
        in_specs=[
            pl.BlockSpec((tm, D_MODEL), lambda i, j: (i, 0)),
            pl.BlockSpec((1, D_MODEL), lambda i, j: (0, 0)),
            pl.BlockSpec((D_MODEL, tn), lambda i, j: (0, j)),
        ],
        out_specs=pl.BlockSpec((tm, tn), lambda i, j: (i, j)),
        out_shape=jax.ShapeDtypeStruct((m, n), BF16),
        scratch_shapes=[pltpu.VMEM((tm, D_MODEL), BF16)],
        compiler_params=_cparams(("parallel", "arbitrary")),
        name="in_proj",
    )(x2, gain, w)


def _qkv_up_body(cq_ref, ckv_ref, kra_ref, krb_ref, gq_ref, gkv_ref, wqt_ref, wk_ref, wvt_ref,
                 cq2_ref, sq2_ref, ck_ref, sk_ref, qt_ref, k_ref, vt_ref, qn2_ref, kn2_ref, *, qscale):
    ts = cq_ref.shape[0]
    cqn = _rms(cq_ref[...].astype(F32), gq_ref[...]).astype(BF16)
    ckvn = _rms(ckv_ref[...].astype(F32), gkv_ref[...]).astype(BF16)
    cq2 = cq2_ref[...]
    sq2 = sq2_ref[...]
    zeros_q = jnp.zeros((MLA_QK_PAD - MLA_NOPE - MLA_ROPE, ts), BF16)
    ones_v = jnp.ones((MLA_ACC_ROWS - MLA_V, ts), BF16)
    ones_n = jnp.ones((8, 128), BF16)
    for h in range(MLA_HEADS):
        qf = _dot_nt(wqt_ref[h], cqn) * qscale
        rope = qf[MLA_NOPE:MLA_NOPE + MLA_ROPE] * cq2 + qf[MLA_NOPE + MLA_ROPE:] * sq2
        qt_ref[0, h, 0:MLA_NOPE, :] = qf[0:MLA_NOPE].astype(BF16)
        qt_ref[0, h, MLA_NOPE:MLA_NOPE + MLA_ROPE, :] = rope.astype(BF16)
        qt_ref[0, h, MLA_NOPE + MLA_ROPE:, :] = zeros_q
        qn2 = (jnp.sum(qf[0:MLA_NOPE] * qf[0:MLA_NOPE], axis=0, keepdims=True)
               + jnp.sum(rope * rope, axis=0, keepdims=True))
        qn2_ref[0, h] = jnp.broadcast_to(qn2, (8, ts))
    kr = kra_ref[...].astype(F32) * ck_ref[...] + krb_ref[...].astype(F32) * sk_ref[...]
    lane = lax.broadcasted_iota(jnp.int32, kr.shape, 1)
    kr_aug = jnp.where(lane == MLA_ROPE, 1.0, kr).astype(BF16)
    kall = _dot(ckvn, wk_ref[...])
    for h in range(MLA_HEADS):
        kh = kall[:, h * MLA_NOPE:(h + 1) * MLA_NOPE]
        k_ref[0, h, :, 0:MLA_NOPE] = kh.astype(BF16)
        k_ref[0, h, :, MLA_NOPE:] = kr_aug
        kn2_ref[0, h] = _dot_nt(ones_n, (kh * kh + kr * kr).astype(BF16))
        vt_ref[0, h, 0:MLA_V, :] = _dot_nt(wvt_ref[h], ckvn).astype(BF16)
        vt_ref[0, h, MLA_V:, :] = ones_v


def _qkv_up(p, gq, gkv, wqt, wk, wvt, cq2, sq2, ck, sk, b, s, ts):
    ns = s // ts
    row = lambda bi, i: bi * ns + i
    qscale = float((MLA_NOPE + MLA_ROPE) ** -0.5 * LOG2E)
    full = lambda shape: pl.BlockSpec(shape, lambda bi, i: (0,) * len(shape))
    return pl.pallas_call(
        functools.partial(_qkv_up_body, qscale=qscale),
        grid=(b, ns),
        in_specs=[
            pl.BlockSpec((ts, MLA_Q_RANK), lambda bi, i: (row(bi, i), COL_CQ // MLA_Q_RANK)),
            pl.BlockSpec((ts, MLA_KV_RANK), lambda bi, i: (row(bi, i), COL_CKV // MLA_KV_RANK)),
            pl.BlockSpec((ts, 128), lambda bi, i: (row(bi, i), COL_KRA // 128)),
            pl.BlockSpec((ts, 128), lambda bi, i: (row(bi, i), COL_KRB // 128)),
            full((1, MLA_Q_RANK)),
            full((1, MLA_KV_RANK)),
            full((MLA_HEADS, MLA_QK_PAD, MLA_Q_RANK)),
            full((MLA_KV_RANK, MLA_HEADS * MLA_NOPE)),
            full((MLA_HEADS, MLA_V, MLA_KV_RANK)),
            pl.BlockSpec((MLA_ROPE, ts), lambda bi, i: (0, i)),
            pl.BlockSpec((MLA_ROPE, ts), lambda bi, i: (0, i)),
            pl.BlockSpec((ts, 128), lambda bi, i: (i, 0)),
            pl.BlockSpec((ts, 128), lambda bi, i: (i, 0)),
        ],
        out_specs=[
            pl.BlockSpec((1, MLA_HEADS, MLA_QK_PAD, ts), lambda bi, i: (bi, 0, 0, i)),
            pl.BlockSpec((1, MLA_HEADS, ts, MLA_QK_PAD), lambda bi, i: (bi, 0, i, 0)),
            pl.BlockSpec((1, MLA_HEADS, MLA_ACC_ROWS, ts), lambda bi, i: (bi, 0, 0, i)),
            pl.BlockSpec((1, MLA_HEADS, 8, ts), lambda bi, i: (bi, 0, 0, i)),
            pl.BlockSpec((1, MLA_HEADS, 8, ts), lambda bi, i: (bi, 0, 0, i)),
        ],
        out_shape=[
            jax.ShapeDtypeStruct((b, MLA_HEADS, MLA_QK_PAD, s), BF16),
            jax.ShapeDtypeStruct((b, MLA_HEADS, s, MLA_QK_PAD), BF16),
            jax.ShapeDtypeStruct((b, MLA_HEADS, MLA_ACC_ROWS, s), BF16),
            jax.ShapeDtypeStruct((b, MLA_HEADS, 8, s), F32),
            jax.ShapeDtypeStruct((b, MLA_HEADS, 8, s), F32),
        ],
        compiler_params=_cparams(("parallel", "parallel")),
        name="qkv_up",
    )(p, p, p, p, gq, gkv, wqt, wk, wvt, cq2, sq2, ck, sk)


MLA_GW = 256
MLA_QK_ROWS = 512
MLA_ACC_ROWS = 144
MLA_SHIFT_ROW = MLA_NOPE + MLA_ROPE
MLA_UB_MARGIN = 1.03125
MLA_L_MIN = 2.0 ** -100


def _mla_body(qt_ref, k_ref, vt_ref, qn2_ref, kn2_ref, o_ref, acc_ref, qs_ref, m_ref, p0, p1, kmax_ref, *, t):
    i = pl.program_id(2)
    n_groups = t // MLA_GW
    p_s = (p0, p1)

    def gcols(g):
        return slice(g * MLA_GW, (g + 1) * MLA_GW)

    @pl.when(i == 0)
    def _():
        kmax2 = jnp.max(jnp.max(kn2_ref[0, 0], axis=1, keepdims=True), axis=0, keepdims=True)
        kmax_ref[...] = jnp.broadcast_to(kmax2, kmax_ref.shape)

    ub = jnp.sqrt(qn2_ref[0, 0, 0:1, :] * kmax_ref[:, 0:1]) * MLA_UB_MARGIN
    qs_ref[...] = qt_ref[0, 0]
    row = lax.broadcasted_iota(jnp.int32, (16, t), 0)
    qs_ref[MLA_SHIFT_ROW:MLA_SHIFT_ROW + 16, :] = jnp.where(row == 0, -ub, 0.0).astype(BF16)
    acc_ref[...] = jnp.zeros_like(acc_ref)

    def probs(k0, rows, g, masked):
        ps = []
        for r in range(0, rows, MLA_QK_ROWS):
            n = min(MLA_QK_ROWS, rows - r)
            st = _dot(k_ref[0, 0, pl.ds(k0 + r, n), :], qs_ref[:, gcols(g)])
            if masked and r + n > rows - MLA_GW:
                kpos = r + lax.broadcasted_iota(jnp.int32, (n, MLA_GW), 0)
                qpos = rows - MLA_GW + lax.broadcasted_iota(jnp.int32, (n, MLA_GW), 1)
                st = jnp.where(kpos <= qpos, st, NEG)
            ps.append(jnp.exp2(st).astype(BF16))
        return ps[0] if len(ps) == 1 else jnp.concatenate(ps, axis=0)

    def key_off(j):
        return pl.multiple_of(jnp.where(j == 0, i, j - 1) * t, t)

    def pv(j, slot, g):
        acc_ref[g] += _dot(vt_ref[0, 0, :, pl.ds(key_off(j), t)], p_s[slot][g])

    def step(k, slot):
        for g in range(n_groups):
            p_s[slot][g] = probs(key_off(k), t, g, False)
            pv(k - 1, 1 - slot, g)

    for g in range(n_groups):
        rows = (g + 1) * MLA_GW
        p0[g, 0:rows, :] = probs(pl.multiple_of(i * t, t), rows, g, True)
        if rows < t:
            p0[g, rows:, :] = jnp.zeros((t - rows, MLA_GW), BF16)

    body_steps = 4 if t <= 1024 else 2

    def body(u, carry):
        for d in range(body_steps):
            step(body_steps * u + d + 1, (d + 1) % 2)
        return carry

    lax.fori_loop(0, i // body_steps, body, 0)

    if body_steps == 4:
        @pl.when((i // 2) % 2 == 1)
        def _():
            k_base = (i // 4) * 4
            step(k_base + 1, 1)
            step(k_base + 2, 0)

    @pl.when(i % 2 == 1)
    def _():
        step(i, 1)
        for g in range(n_groups):
            pv(i, 1, g)

    @pl.when(i % 2 == 0)
    def _():
        for g in range(n_groups):
            pv(i, 0, g)

    l_min = acc_ref[0, MLA_V:MLA_V + 1, :]
    for g in range(1, n_groups):
        l_min = jnp.minimum(l_min, acc_ref[g, MLA_V:MLA_V + 1, :])
    l_min = jnp.min(l_min, axis=1, keepdims=True)

    @pl.when(jnp.logical_not(l_min[0, 0] >= MLA_L_MIN))
    def _():
        acc_ref[...] = jnp.zeros_like(acc_ref)
        m_ref[...] = jnp.full_like(m_ref, NEG)

        def exact_chunk(c, carry):
            k0 = pl.multiple_of(c * MLA_QK_ROWS, MLA_QK_ROWS)
            kc = k_ref[0, 0, pl.ds(k0, MLA_QK_ROWS), :]
            vt1 = vt_ref[0, 0, :, pl.ds(k0, MLA_QK_ROWS)]
            for g in range(n_groups):
                st = _dot(kc, qt_ref[0, 0, :, gcols(g)])
                kpos = k0 + lax.broadcasted_iota(jnp.int32, st.shape, 0)
                qpos = i * t + g * MLA_GW + lax.broadcasted_iota(jnp.int32, st.shape, 1)
                st = jnp.where(kpos <= qpos, st, NEG)
                m_old = m_ref[g]
                m_new = jnp.maximum(m_old, jnp.max(st, axis=0, keepdims=True))
                m_ref[g] = m_new
                p = jnp.exp2(st - m_new).astype(BF16)
                acc_ref[g] = jnp.exp2(m_old - m_new) * acc_ref[g] + _dot(vt1, p)
            return carry

        lax.fori_loop(0, (i + 1) * (t // MLA_QK_ROWS), exact_chunk, 0)

    for g in range(n_groups):
        out_t = acc_ref[g, 0:MLA_V, :] / acc_ref[g, MLA_V:MLA_V + 1, :]
        o_ref[0, g * MLA_GW:(g + 1) * MLA_GW, :] = out_t.T.astype(o_ref.dtype)


def _mla_attention(qt, k, vt, qn2, kn2, t):
    b, h, _, s = qt.shape
    n_groups = t // MLA_GW
    return pl.pallas_call(
        functools.partial(_mla_body, t=t),
        grid=(b, h, s // t),
        in_specs=[
            pl.BlockSpec((1, 1, MLA_QK_PAD, t), lambda bi, hi, i: (bi, hi, 0, i)),
            pl.BlockSpec((1, 1, s, MLA_QK_PAD), lambda bi, hi, i: (bi, hi, 0, 0)),
            pl.BlockSpec((1, 1, MLA_ACC_ROWS, s), lambda bi, hi, i: (bi, hi, 0, 0)),
            pl.BlockSpec((1, 1, 8, t), lambda bi, hi, i: (bi, hi, 0, i)),
            pl.BlockSpec((1, 1, 8, s), lambda bi, hi, i: (bi, hi, 0, 0)),
        ],
        out_specs=pl.BlockSpec((1, t, MLA_V), lambda bi, hi, i: (bi, i, hi)),
        out_shape=jax.ShapeDtypeStruct((b, s, h * MLA_V), BF16),
        scratch_shapes=[
            pltpu.VMEM((n_groups, MLA_ACC_ROWS, MLA_GW), F32),
            pltpu.VMEM((MLA_QK_PAD, t), BF16),
            pltpu.VMEM((n_groups, 1, MLA_GW), F32),
            pltpu.VMEM((n_groups, t, MLA_GW), BF16),
            pltpu.VMEM((n_groups, t, MLA_GW), BF16),
            pltpu.VMEM((1, 128), F32),
        ],
        compiler_params=_cparams(("parallel", "parallel", "arbitrary")),
        name="mla_attn",
    )(qt, k, vt, qn2, kn2)


def _t5_bucket_table():
    n = np.arange(REL_MAX_DIST)
    max_exact = REL_BUCKETS // 2
    large = max_exact + (np.log(np.maximum(n, 1).astype(np.float32) / max_exact)
                         / math.log(REL_MAX_DIST / max_exact) * (REL_BUCKETS - max_exact)).astype(np.int32)
    large = np.minimum(large, REL_BUCKETS - 1)
    return np.where(n < max_exact, n, large)


def _swa_bias_body(bucket_ref, table_ref, o_ref):
    bucket = bucket_ref[...]
    kj = lax.broadcasted_iota(jnp.int32, bucket.shape, 1)
    for h in range(SWA_Q_HEADS):
        bias = jnp.zeros(bucket.shape, F32)
        for bk in range(REL_BUCKETS):
            bias = jnp.where(bucket == bk, table_ref[bk, h], bias)
        valid = bucket >= 0
        bias = bias * LOG2E
        o_ref[1, h] = jnp.where(valid, bias, NEG)
        o_ref[0, h] = jnp.where(valid & (kj >= SWA_BLOCK), bias, NEG)


def _swa_bias(rel_table):
    qi = np.arange(SWA_BLOCK)[:, None]
    kj = np.arange(2 * SWA_BLOCK)[None, :]
    dist = qi + SWA_BLOCK - kj
    in_window = (dist >= 0) & (dist < SWA_BLOCK)
    bucket = np.where(in_window, _t5_bucket_table()[np.clip(dist, 0, REL_MAX_DIST - 1)], -1).astype(np.int32)
    return pl.pallas_call(
        _swa_bias_body,
        in_specs=[
            pl.BlockSpec(memory_space=pltpu.VMEM),
            pl.BlockSpec(memory_space=pltpu.SMEM),
        ],
        out_specs=pl.BlockSpec(memory_space=pltpu.VMEM),
        out_shape=jax.ShapeDtypeStruct((2, SWA_Q_HEADS, SWA_BLOCK, 2 * SWA_BLOCK), F32),
        name="swa_bias",
    )(jnp.asarray(bucket), rel_table)


def _swa_body(q_ref, kc_ref, kp_ref, vc_ref, vp_ref, bias_ref, sink_ref, o_ref):
    blk = SWA_BLOCK
    lane = lax.broadcasted_iota(jnp.int32, (2 * blk, 128), 1)
    lane_q = lax.broadcasted_iota(jnp.int32, (blk, 128), 1)
    kband = jnp.concatenate([kp_ref[...], kc_ref[...]], axis=0)
    vband = jnp.concatenate([vp_ref[...], vc_ref[...]], axis=0)

    def dup_half(x2, half):
        rolled = pltpu.roll(x2, 64, axis=1)
        first = lane < 64
        return jnp.where(first, x2, rolled) if half == 0 else jnp.where(first, rolled, x2)

    for kvh in range(SWA_KV_HEADS):
        pair_cols = slice((kvh // 2) * 128, (kvh // 2 + 1) * 128)
        k2 = dup_half(kband[:, pair_cols].astype(F32), kvh % 2).astype(BF16)
        v2 = dup_half(vband[:, pair_cols].astype(F32), kvh % 2).astype(BF16)
        parts = []
        for j in range(SWA_GROUP // 2):
            c0 = kvh * SWA_GROUP * SWA_DH + j * 128
            q2 = q_ref[:, c0:c0 + 128]
            zero = jnp.zeros_like(q2)
            parts.append(jnp.where(lane_q < 64, q2, zero))
            parts.append(jnp.where(lane_q < 64, zero, q2))
        qstack = jnp.concatenate(parts, axis=0)
        s_all = _dot_nt(qstack, k2)
        ps = []
        dens = []
        for g in range(SWA_GROUP):
            hq = kvh * SWA_GROUP + g
            sg = s_all[g * blk:(g + 1) * blk] + bias_ref[0, hq]
            sink = sink_ref[hq] * LOG2E
            m = jnp.maximum(jnp.max(sg, axis=-1, keepdims=True), sink)
            p = jnp.exp2(sg - m)
            dens.append(jnp.sum(p, axis=-1, keepdims=True) + jnp.exp2(sink - m))
            ps.append(p.astype(BF16))
        o_all = _dot(jnp.concatenate(ps, axis=0), v2)
        for j in range(SWA_GROUP // 2):
            oa = o_all[(2 * j) * blk:(2 * j + 1) * blk] / dens[2 * j]
            ob = o_all[(2 * j + 1) * blk:(2 * j + 2) * blk] / dens[2 * j + 1]
            c0 = kvh * SWA_GROUP * SWA_DH + j * 128
            o_ref[:, c0:c0 + 128] = jnp.where(lane_q < 64, oa, ob).astype(o_ref.dtype)


def _swa_attention(p, bias, sinks, b, s):
    nb = s // SWA_BLOCK
    kvw = SWA_KV_HEADS * SWA_DH
    row = lambda bi, n: bi * nb + n
    prow = lambda bi, n: bi * nb + jnp.maximum(n - 1, 0)
    return pl.pallas_call(
        _swa_body,
        grid=(b, nb),
        in_specs=[
            pl.BlockSpec((SWA_BLOCK, D_MODEL), lambda bi, n: (row(bi, n), COL_QS // D_MODEL)),
            pl.BlockSpec((SWA_BLOCK, kvw), lambda bi, n: (row(bi, n), COL_KS // kvw)),
            pl.BlockSpec((SWA_BLOCK, kvw), lambda bi, n: (prow(bi, n), COL_KS // kvw)),
            pl.BlockSpec((SWA_BLOCK, kvw), lambda bi, n: (row(bi, n), COL_VS // kvw)),
            pl.BlockSpec((SWA_BLOCK, kvw), lambda bi, n: (prow(bi, n), COL_VS // kvw)),
            pl.BlockSpec((1, SWA_Q_HEADS, SWA_BLOCK, 2 * SWA_BLOCK),
                         lambda bi, n: (jnp.minimum(n, 1), 0, 0, 0)),
            pl.BlockSpec(memory_space=pltpu.SMEM),
        ],
        out_specs=pl.BlockSpec((SWA_BLOCK, D_MODEL), lambda bi, n: (row(bi, n), 0)),
        out_shape=jax.ShapeDtypeStruct((b * s, D_MODEL), BF16),
        compiler_params=_cparams(("parallel", "arbitrary")),
        name="swa_attn",
    )(p, p, p, p, p, bias, sinks)


def _merge_body(oa_ref, ob_ref, wa_ref, wb_ref, g0_ref, g1_ref, o_ref):
    ya = _dot(oa_ref[...], wa_ref[...])
    yb = _dot(ob_ref[...], wb_ref[...])
    g0 = jax.nn.sigmoid(g0_ref[...].astype(F32))
    g1 = jax.nn.sigmoid(g1_ref[...].astype(F32))
    o_ref[...] = (g0 * ya + g1 * yb).astype(o_ref.dtype)


def _merge(oa, ob, wa, wb, p, tm, tn):
    m = oa.shape[0]
    nj = D_MODEL // tn
    return pl.pallas_call(
        _merge_body,
        grid=(m // tm, nj),
        in_specs=[
            pl.BlockSpec((tm, D_MODEL), lambda i, j: (i, 0)),
            pl.BlockSpec((tm, D_MODEL), lambda i, j: (i, 0)),
            pl.BlockSpec((D_MODEL, tn), lambda i, j: (0, j)),
            pl.BlockSpec((D_MODEL, tn), lambda i, j: (0, j)),
            pl.BlockSpec((tm, tn), lambda i, j: (i, COL_GATES // tn + j)),
            pl.BlockSpec((tm, tn), lambda i, j: (i, COL_GATES // tn + nj + j)),
        ],
        out_specs=pl.BlockSpec((tm, tn), lambda i, j: (i, j)),
        out_shape=jax.ShapeDtypeStruct((m, D_MODEL), BF16),
        compiler_params=_cparams(("parallel", "arbitrary")),
        name="gated_merge",
    )(oa, ob, wa, wb, p, p)


def _out_proj_body(mg_ref, w_ref, x_ref, g_ref, o_ref):
    y = _dot(mg_ref[...], w_ref[...])
    o_ref[...] = x_ref[...] + _rms(y, g_ref[...])


def _out_proj(mg, w, x2, gain, tm):
    m = mg.shape[0]
    return pl.pallas_call(
        _out_proj_body,
        grid=(m // tm,),
        in_specs=[
            pl.BlockSpec((tm, D_MODEL), lambda i: (i, 0)),
            pl.BlockSpec((D_MODEL, D_MODEL), lambda i: (0, 0)),
            pl.BlockSpec((tm, D_MODEL), lambda i: (i, 0)),
            pl.BlockSpec((1, D_MODEL), lambda i: (0, 0)),
        ],
        out_specs=pl.BlockSpec((tm, D_MODEL), lambda i: (i, 0)),
        out_shape=jax.ShapeDtypeStruct((m, D_MODEL), F32),
        compiler_params=_cparams(("parallel",)),
        name="out_proj",
    )(mg, w, x2, gain)


HALO = 16


def _ffn_body(x_ref, xh_ref, gpre_ref, wa_ref, wb_ref, cw_ref, cb_ref, wd_ref, gpost_ref, o_ref,
              hn_ref, a_ref, acc_ref, *, tiles_per_seq):
    i = pl.program_id(0)
    j = pl.program_id(1)
    tm = x_ref.shape[0]

    @pl.when(j == 0)
    def _():
        halo = _rms(xh_ref[...], gpre_ref[...])
        halo = jnp.where(i % tiles_per_seq == 0, jnp.zeros_like(halo), halo)
        hn_ref[0:HALO, :] = halo.astype(BF16)
        hn_ref[HALO:, :] = _rms(x_ref[...], gpre_ref[...]).astype(BF16)
        acc_ref[...] = jnp.zeros_like(acc_ref)

    a_ref[...] = _dot(hn_ref[...], wa_ref[...])
    gate = _dot(hn_ref[HALO:, :], wb_ref[...])
    c = cb_ref[...] + cw_ref[0:1, :] * a_ref[pl.ds(HALO - 2, tm), :]
    c = c + cw_ref[1:2, :] * a_ref[pl.ds(HALO - 1, tm), :]
    c = c + cw_ref[2:3, :] * a_ref[pl.ds(HALO, tm), :]
    act = (jax.nn.gelu(c, approximate=True) * gate).astype(BF16)
    acc_ref[...] += _dot(act, wd_ref[...])

    @pl.when(j == pl.num_programs(1) - 1)
    def _():
        o_ref[...] = x_ref[...] + _rms(acc_ref[...], gpost_ref[...])


def _ffn(x1, gain_pre, w_up, conv_w, conv_b, w_down, gain_post, s, tm, tn):
    m = x1.shape[0]
    nj = D_FF // tn
    hb = tm // HALO
    return pl.pallas_call(
        functools.partial(_ffn_body, tiles_per_seq=s // tm),
        grid=(m // tm, nj),
        in_specs=[
            pl.BlockSpec((tm, D_MODEL), lambda i, j: (i, 0)),
            pl.BlockSpec((HALO, D_MODEL), lambda i, j: (jnp.maximum(i * hb - 1, 0), 0)),
            pl.BlockSpec((1, D_MODEL), lambda i, j: (0, 0)),
            pl.BlockSpec((D_MODEL, tn), lambda i, j: (0, j)),
            pl.BlockSpec((D_MODEL, tn), lambda i, j: (0, nj + j)),
            pl.BlockSpec((3, tn), lambda i, j: (0, j)),
            pl.BlockSpec((1, tn), lambda i, j: (0, j)),
            pl.BlockSpec((tn, D_MODEL), lambda i, j: (j, 0)),
            pl.BlockSpec((1, D_MODEL), lambda i, j: (0, 0)),
        ],
        out_specs=pl.BlockSpec((tm, D_MODEL), lambda i, j: (i, 0)),
        out_shape=jax.ShapeDtypeStruct((m, D_MODEL), F32),
        scratch_shapes=[
            pltpu.VMEM((HALO + tm, D_MODEL), BF16),
            pltpu.VMEM((HALO + tm, tn), F32),
            pltpu.VMEM((tm, D_MODEL), F32),
        ],
        compiler_params=_cparams(("parallel", "arbitrary")),
        name="ffn_fused",
    )(x1, x1, gain_pre, w_up, w_up, conv_w, conv_b, w_down, gain_post)


def _prep_w_in(w_in):
    cq = w_in[:, 0:512]
    ckv = w_in[:, 512:768]
    kr = w_in[:, 768:832]
    qs = w_in[:, 832:2880] * SWA_QSCALE
    ks = w_in[:, 2880:3136]
    vs = w_in[:, 3136:3392]
    gates = w_in[:, 3392:7488]
    z64 = jnp.zeros((D_MODEL, 64), BF16)
    kr_sw = jnp.concatenate([kr[:, 32:], kr[:, :32]], axis=1)
    parts = [gates, qs, cq, ks, vs, ckv, kr, z64, kr_sw, z64]
    return jnp.concatenate([x.astype(BF16) for x in parts], axis=1)


def _prep_w_q_up(w):
    w = w.reshape(MLA_Q_RANK, MLA_HEADS, MLA_NOPE + MLA_ROPE)
    nope = w[:, :, :MLA_NOPE]
    rope = w[:, :, MLA_NOPE:]
    rope_sw = jnp.concatenate([rope[:, :, 32:], rope[:, :, :32]], axis=2)
    wq = jnp.concatenate([nope, rope, rope_sw], axis=2)
    return jnp.transpose(wq, (1, 2, 0)).astype(BF16)


def _prep_w_kv_up(w):
    w = w.reshape(MLA_KV_RANK, MLA_HEADS, MLA_NOPE + MLA_V)
    wk = w[:, :, :MLA_NOPE].reshape(MLA_KV_RANK, MLA_HEADS * MLA_NOPE).astype(BF16)
    wvt = jnp.transpose(w[:, :, MLA_NOPE:], (1, 2, 0)).astype(BF16)
    return wk, wvt


def _rope_tables(s):
    inv = ROPE_THETA ** (-jnp.arange(0, MLA_ROPE, 2, dtype=F32) / MLA_ROPE)
    ang = jnp.arange(s).astype(F32)[:, None] * inv[None, :]
    cos, sin = jnp.cos(ang), jnp.sin(ang)
    z = jnp.zeros((s, 64), F32)
    ck = jnp.concatenate([cos, cos, z], axis=1)
    sk = jnp.concatenate([-sin, sin, z], axis=1)
    cq2 = jnp.concatenate([cos, cos], axis=1).T
    sq2 = jnp.concatenate([-sin, sin], axis=1).T
    return cq2, sq2, ck, sk


def _tile(n, pref):
    t = min(n, pref)
    assert n % t == 0, (n, pref)
    return t


TILE_IN_PROJ = (1024, 1280)
TILE_QKV_ROWS = 512
TILE_MLA = 2048
TILE_MERGE = (1024, 1024)
TILE_OUT_ROWS = 512
TILE_FFN = (512, 512)


def kernel(x, norm_mix_pre, norm_mix_post, norm_ffn_pre, norm_ffn_post, w_in, mla_q_norm, mla_w_q_up, mla_kv_norm, mla_w_kv_up, swa_sinks, rel_bias_table, w_o_mla, w_o_swa, w_out, ffn_w_up, ffn_conv_w, ffn_conv_b, ffn_w_down):
    b, s, d = x.shape
    assert d == D_MODEL and s % SWA_BLOCK == 0
    depth = w_in.shape[0]
    m = b * s
    x2 = x.reshape(m, d)
    cq2, sq2, ck, sk = _rope_tables(s)
    bias = _swa_bias(rel_bias_table.astype(F32))
    for l in range(depth):
        p = _in_proj(x2, norm_mix_pre[l][None], _prep_w_in(w_in[l]), _tile(m, TILE_IN_PROJ[0]), TILE_IN_PROJ[1])
        wk, wvt = _prep_w_kv_up(mla_w_kv_up[l])
        qt, k, vt, qn2, kn2 = _qkv_up(p, mla_q_norm[l][None], mla_kv_norm[l][None], _prep_w_q_up(mla_w_q_up[l]),
                                      wk, wvt, cq2, sq2, ck, sk, b, s, _tile(s, TILE_QKV_ROWS))
        o_a = _mla_attention(qt, k, vt, qn2, kn2, _tile(s, TILE_MLA)).reshape(m, d)
        o_b = _swa_attention(p, bias, swa_sinks[l].astype(F32), b, s)
        mg = _merge(o_a, o_b, w_o_mla[l].astype(BF16), w_o_swa[l].astype(BF16), p,
                    _tile(m, TILE_MERGE[0]), TILE_MERGE[1])
        x1 = _out_proj(mg, w_out[l].astype(BF16), x2, norm_mix_post[l][None], _tile(m, TILE_OUT_ROWS))
        x2 = _ffn(x1, norm_ffn_pre[l][None], ffn_w_up[l].astype(BF16), ffn_conv_w[l], ffn_conv_b[l][None],
                  ffn_w_down[l].astype(BF16), norm_ffn_post[l][None], s, _tile(s, TILE_FFN[0]), TILE_FFN[1])
    return x2.reshape(b, s, d)
```

```python
import functools
import math

import numpy as np
import jax
import jax.numpy as jnp
from jax import lax
from jax.experimental import pallas as pl
from jax.experimental.pallas import tpu as pltpu

F32 = jnp.float32
BF16 = jnp.bfloat16

D_MODEL = 2048
MLA_HEADS = 16
MLA_Q_RANK = 512
MLA_KV_RANK = 256
MLA_NOPE = 128
MLA_ROPE = 64
MLA_V = 128
MLA_QK_PAD = 256
ROPE_THETA = 10000.0
SWA_Q_HEADS = 32
SWA_KV_HEADS = 4
SWA_GROUP = 8
SWA_DH = 64
SWA_BLOCK = 128
REL_BUCKETS = 32
REL_MAX_DIST = 128
D_FF = 5632
EPS = 1e-6
NEG = -1e30
LOG2E = 1.4426950408889634
SWA_QSCALE = SWA_DH ** -0.5 * LOG2E

COL_GATES = 0
COL_QS = 4096
COL_CQ = 6144
COL_KS = 6656
COL_VS = 6912
COL_CKV = 7168
COL_KRA = 7424
COL_KRB = 7552
IN_COLS_PAD = 7680

VMEM_LIMIT = 56 * 1024 * 1024


def _cparams(sem, flags=None):
    return pltpu.CompilerParams(dimension_semantics=sem, vmem_limit_bytes=VMEM_LIMIT, flags=flags)


def _rms(x, g):
    return x * lax.rsqrt(jnp.mean(x * x, axis=-1, keepdims=True) + EPS) * g


def _dot(a, b):
    return jnp.dot(a, b, preferred_element_type=F32)


def _dot_nt(a, b):
    return lax.dot_general(a, b, (((1,), (1,)), ((), ())), preferred_element_type=F32)


def _in_proj_body(x_ref, g_ref, w_ref, o_ref, hn_ref):
    @pl.when(pl.program_id(1) == 0)
    def _():
        hn_ref[...] = _rms(x_ref[...], g_ref[...]).astype(BF16)

    o_ref[...] = _dot(hn_ref[...], w_ref[...]).astype(o_ref.dtype)


def _in_proj(x2, gain, w, tm, tn):
    m = x2.shape[0]
    n = w.shape[1]
    return pl.pallas_call(
        _in_proj_body,
        grid=(m // tm, n // tn),
        in_specs=[
            pl.BlockSpec((tm, D_MODEL), lambda i, j: (i, 0)),
            pl.BlockSpec((1, D_MODEL), lambda i, j: (0, 0)),
            pl.BlockSpec((D_MODEL, tn), lambda i, j: (0, j)),
        ],
        out_specs=pl.BlockSpec((tm, tn), lambda i, j: (i, j)),
        out_shape=jax.ShapeDtypeStruct((m, n), BF16),
        scratch_shapes=[pltpu.VMEM((tm, D_MODEL), BF16)],
        compiler_params=_cparams(("parallel", "arbitrary")),
        name="in_proj",
    )(x2, gain, w)


def _qkv_up_body(cq_ref, ckv_ref, kra_ref, krb_ref, gq_ref, gkv_ref, wqt_ref, wk_ref, wvt_ref,
                 cq2_ref, sq2_ref, ck_ref, sk_ref, qt_ref, k_ref, vt_ref, qn2_ref, kn2_ref, *, qscale):
    ts = cq_ref.shape[0]
    cqn = _rms(cq_ref[...].astype(F32), gq_ref[...]).astype(BF16)
    ckvn = _rms(ckv_ref[...].astype(F32), gkv_ref[...]).astype(BF16)
    cq2 = cq2_ref[...]
    sq2 = sq2_ref[...]
    zeros_q = jnp.zeros((MLA_QK_PAD - MLA_NOPE - MLA_ROPE, ts), BF16)
    ones_v = jnp.ones((MLA_ACC_ROWS - MLA_V, ts), BF16)
    ones_n = jnp.ones((8, 128), BF16)
    for h in range(MLA_HEADS):
        qf = _dot_nt(wqt_ref[h], cqn) * qscale
        rope = qf[MLA_NOPE:MLA_NOPE + MLA_ROPE] * cq2 + qf[MLA_NOPE + MLA_ROPE:] * sq2
        qt_ref[0, h, 0:MLA_NOPE, :] = qf[0:MLA_NOPE].astype(BF16)
        qt_ref[0, h, MLA_NOPE:MLA_NOPE + MLA_ROPE, :] = rope.astype(BF16)
        qt_ref[0, h, MLA_NOPE + MLA_ROPE:, :] = zeros_q
        qn2 = (jnp.sum(qf[0:MLA_NOPE] * qf[0:MLA_NOPE], axis=0, keepdims=True)
               + jnp.sum(rope * rope, axis=0, keepdims=True))
        qn2_ref[0, h] = jnp.broadcast_to(qn2, (8, ts))
    kr = kra_ref[...].astype(F32) * ck_ref[...] + krb_ref[...].astype(F32) * sk_ref[...]
    lane = lax.broadcasted_iota(jnp.int32, kr.shape, 1)
    kr_aug = jnp.where(lane == MLA_ROPE, 1.0, kr).astype(BF16)
    kall = _dot(ckvn, wk_ref[...])
    for h in range(MLA_HEADS):
        kh = kall[:, h * MLA_NOPE:(h + 1) * MLA_NOPE]
        k_ref[0, h, :, 0:MLA_NOPE] = kh.astype(BF16)
        k_ref[0, h, :, MLA_NOPE:] = kr_aug
        kn2_ref[0, h] = _dot_nt(ones_n, (kh * kh + kr * kr).astype(BF16))
        vt_ref[0, h, 0:MLA_V, :] = _dot_nt(wvt_ref[h], ckvn).astype(BF16)
        vt_ref[0, h, MLA_V:, :] = ones_v


def _qkv_up(p, gq, gkv, wqt, wk, wvt, cq2, sq2, ck, sk, b, s, ts):
    ns = s // ts
    row = lambda bi, i: bi * ns + i
    qscale = float((MLA_NOPE + MLA_ROPE) ** -0.5 * LOG2E)
    full = lambda shape: pl.BlockSpec(shape, lambda bi, i: (0,) * len(shape))
    return pl.pallas_call(
        functools.partial(_qkv_up_body, qscale=qscale),
        grid=(b, ns),
        in_specs=[
            pl.BlockSpec((ts, MLA_Q_RANK), lambda bi, i: (row(bi, i), COL_CQ // MLA_Q_RANK)),
            pl.BlockSpec((ts, MLA_KV_RANK), lambda bi, i: (row(bi, i), COL_CKV // MLA_KV_RANK)),
            pl.BlockSpec((ts, 128), lambda bi, i: (row(bi, i), COL_KRA // 128)),
            pl.BlockSpec((ts, 128), lambda bi, i: (row(bi, i), COL_KRB // 128)),
            full((1, MLA_Q_RANK)),
            full((1, MLA_KV_RANK)),
            full((MLA_HEADS, MLA_QK_PAD, MLA_Q_RANK)),
            full((MLA_KV_RANK, MLA_HEADS * MLA_NOPE)),
            full((MLA_HEADS, MLA_V, MLA_KV_RANK)),
            pl.BlockSpec((MLA_ROPE, ts), lambda bi, i: (0, i)),
            pl.BlockSpec((MLA_ROPE, ts), lambda bi, i: (0, i)),
            pl.BlockSpec((ts, 128), lambda bi, i: (i, 0)),
            pl.BlockSpec((ts, 128), lambda bi, i: (i, 0)),
        ],
        out_specs=[
            pl.BlockSpec((1, MLA_HEADS, MLA_QK_PAD, ts), lambda bi, i: (bi, 0, 0, i)),
            pl.BlockSpec((1, MLA_HEADS, ts, MLA_QK_PAD), lambda bi, i: (bi, 0, i, 0)),
            pl.BlockSpec((1, MLA_HEADS, MLA_ACC_ROWS, ts), lambda bi, i: (bi, 0, 0, i)),
            pl.BlockSpec((1, MLA_HEADS, 8, ts), lambda bi, i: (bi, 0, 0, i)),
            pl.BlockSpec((1, MLA_HEADS, 8, ts), lambda bi, i: (bi, 0, 0, i)),
        ],
        out_shape=[
            jax.ShapeDtypeStruct((b, MLA_HEADS, MLA_QK_PAD, s), BF16),
            jax.ShapeDtypeStruct((b, MLA_HEADS, s, MLA_QK_PAD), BF16),
            jax.ShapeDtypeStruct((b, MLA_HEADS, MLA_ACC_ROWS, s), BF16),
            jax.ShapeDtypeStruct((b, MLA_HEADS, 8, s), F32),
            jax.ShapeDtypeStruct((b, MLA_HEADS, 8, s), F32),
        ],
        compiler_params=_cparams(("parallel", "parallel")),
        name="qkv_up",
    )(p, p, p, p, gq, gkv, wqt, wk, wvt, cq2, sq2, ck, sk)


MLA_GW = 256
MLA_QK_ROWS = 512
MLA_ACC_ROWS = 144
MLA_SHIFT_ROW = MLA_NOPE + MLA_ROPE
MLA_UB_MARGIN = 1.03125
MLA_L_MIN = 2.0 ** -100


def _mla_body(qt_ref, k_ref, vt_ref, qn2_ref, kn2_ref, o_ref, acc_ref, qs_ref, m_ref, p0, p1, kmax_ref, *, t):
    i = pl.program_id(2)
    n_groups = t // MLA_GW
    p_s = (p0, p1)

    def gcols(g):
        return slice(g * MLA_GW, (g + 1) * MLA_GW)

    @pl.when(i == 0)
    def _():
        kmax2 = jnp.max(jnp.max(kn2_ref[0, 0], axis=1, keepdims=True), axis=0, keepdims=True)
        kmax_ref[...] = jnp.broadcast_to(kmax2, kmax_ref.shape)

    ub = jnp.sqrt(qn2_ref[0, 0, 0:1, :] * kmax_ref[:, 0:1]) * MLA_UB_MARGIN
    qs_ref[...] = qt_ref[0, 0]
    row = lax.broadcasted_iota(jnp.int32, (16, t), 0)
    qs_ref[MLA_SHIFT_ROW:MLA_SHIFT_ROW + 16, :] = jnp.where(row == 0, -ub, 0.0).astype(BF16)
    acc_ref[...] = jnp.zeros_like(acc_ref)

    def probs(k0, rows, g, masked):
        ps = []
        for r in range(0, rows, MLA_QK_ROWS):
            n = min(MLA_QK_ROWS, rows - r)
            st = _dot(k_ref[0, 0, pl.ds(k0 + r, n), :], qs_ref[:, gcols(g)])
            if masked and r + n > rows - MLA_GW:
                kpos = r + lax.broadcasted_iota(jnp.int32, (n, MLA_GW), 0)
                qpos = rows - MLA_GW + lax.broadcasted_iota(jnp.int32, (n, MLA_GW), 1)
                st = jnp.where(kpos <= qpos, st, NEG)
            ps.append(jnp.exp2(st).astype(BF16))
        return ps[0] if len(ps) == 1 else jnp.concatenate(ps, axis=0)

    def key_off(j):
        return pl.multiple_of(jnp.where(j == 0, i, j - 1) * t, t)

    def pv(j, slot, g):
        acc_ref[g] += _dot(vt_ref[0, 0, :, pl.ds(key_off(j), t)], p_s[slot][g])

    def step(k, slot):
        for g in range(n_groups):
            p_s[slot][g] = probs(key_off(k), t, g, False)
            pv(k - 1, 1 - slot, g)

    for g in range(n_groups):
        rows = (g + 1) * MLA_GW
        p0[g, 0:rows, :] = probs(pl.multiple_of(i * t, t), rows, g, True)
        if rows < t:
            p0[g, rows:, :] = jnp.zeros((t - rows, MLA_GW), BF16)

    body_steps = 4 if t <= 1024 else 2

    def body(u, carry):
        for d in range(body_steps):
            step(body_steps * u + d + 1, (d + 1) % 2)
        return carry

    lax.fori_loop(0, i // body_steps, body, 0)

    if body_steps == 4:
        @pl.when((i // 2) % 2 == 1)
        def _():
            k_base = (i // 4) * 4
            step(k_base + 1, 1)
            step(k_base + 2, 0)

    @pl.when(i % 2 == 1)
    def _():
        step(i, 1)
        for g in range(n_groups):
            pv(i, 1, g)

    @pl.when(i % 2 == 0)
    def _():
        for g in range(n_groups):
            pv(i, 0, g)

    l_min = acc_ref[0, MLA_V:MLA_V + 1, :]
    for g in range(1, n_groups):
        l_min = jnp.minimum(l_min, acc_ref[g, MLA_V:MLA_V + 1, :])
    l_min = jnp.min(l_min, axis=1, keepdims=True)

    @pl.when(jnp.logical_not(l_min[0, 0] >= MLA_L_MIN))
    def _():
        acc_ref[...] = jnp.zeros_like(acc_ref)
        m_ref[...] = jnp.full_like(m_ref, NEG)

        def exact_chunk(c, carry):
            k0 = pl.multiple_of(c * MLA_QK_ROWS, MLA_QK_ROWS)
            kc = k_ref[0, 0, pl.ds(k0, MLA_QK_ROWS), :]
            vt1 = vt_ref[0, 0, :, pl.ds(k0, MLA_QK_ROWS)]
            for g in range(n_groups):
                st = _dot(kc, qt_ref[0, 0, :, gcols(g)])
                kpos = k0 + lax.broadcasted_iota(jnp.int32, st.shape, 0)
                qpos = i * t + g * MLA_GW + lax.broadcasted_iota(jnp.int32, st.shape, 1)
                st = jnp.where(kpos <= qpos, st, NEG)
                m_old = m_ref[g]
                m_new = jnp.maximum(m_old, jnp.max(st, axis=0, keepdims=True))
                m_ref[g] = m_new
                p = jnp.exp2(st - m_new).astype(BF16)
                acc_ref[g] = jnp.exp2(m_old - m_new) * acc_ref[g] + _dot(vt1, p)
            return carry

        lax.fori_loop(0, (i + 1) * (t // MLA_QK_ROWS), exact_chunk, 0)

    for g in range(n_groups):
        out_t = acc_ref[g, 0:MLA_V, :] / acc_ref[g, MLA_V:MLA_V + 1, :]
        o_ref[0, g * MLA_GW:(g + 1) * MLA_GW, :] = out_t.T.astype(o_ref.dtype)


def _mla_attention(qt, k, vt, qn2, kn2, t):
    b, h, _, s = qt.shape
    n_groups = t // MLA_GW
    return pl.pallas_call(
        functools.partial(_mla_body, t=t),
        grid=(b, h, s // t),
        in_specs=[
            pl.BlockSpec((1, 1, MLA_QK_PAD, t), lambda bi, hi, i: (bi, hi, 0, i)),
            pl.BlockSpec((1, 1, s, MLA_QK_PAD), lambda bi, hi, i: (bi, hi, 0, 0)),
            pl.BlockSpec((1, 1, MLA_ACC_ROWS, s), lambda bi, hi, i: (bi, hi, 0, 0)),
            pl.BlockSpec((1, 1, 8, t), lambda bi, hi, i: (bi, hi, 0, i)),
            pl.BlockSpec((1, 1, 8, s), lambda bi, hi, i: (bi, hi, 0, 0)),
        ],
        out_specs=pl.BlockSpec((1, t, MLA_V), lambda bi, hi, i: (bi, i, hi)),
        out_shape=jax.ShapeDtypeStruct((b, s, h * MLA_V), BF16),
        scratch_shapes=[
            pltpu.VMEM((n_groups, MLA_ACC_ROWS, MLA_GW), F32),
            pltpu.VMEM((MLA_QK_PAD, t), BF16),
            pltpu.VMEM((n_groups, 1, MLA_GW), F32),
            pltpu.VMEM((n_groups, t, MLA_GW), BF16),
            pltpu.VMEM((n_groups, t, MLA_GW), BF16),
            pltpu.VMEM((1, 128), F32),
        ],
        compiler_params=_cparams(("parallel", "parallel", "arbitrary")),
        name="mla_attn",
    )(qt, k, vt, qn2, kn2)


def _t5_bucket_table():
    n = np.arange(REL_MAX_DIST)
    max_exact = REL_BUCKETS // 2
    large = max_exact + (np.log(np.maximum(n, 1).astype(np.float32) / max_exact)
                         / math.log(REL_MAX_DIST / max_exact) * (REL_BUCKETS - max_exact)).astype(np.int32)
    large = np.minimum(large, REL_BUCKETS - 1)
    return np.where(n < max_exact, n, large)


def _swa_bias_body(bucket_ref, table_ref, o_ref):
    bucket = bucket_ref[...]
    kj = lax.broadcasted_iota(jnp.int32, bucket.shape, 1)
    for h in range(SWA_Q_HEADS):
        bias = jnp.zeros(bucket.shape, F32)
        for bk in range(REL_BUCKETS):
            bias = jnp.where(bucket == bk, table_ref[bk, h], bias)
        valid = bucket >= 0
        bias = bias * LOG2E
        o_ref[1, h] = jnp.where(valid, bias, NEG)
        o_ref[0, h] = jnp.where(valid & (kj >= SWA_BLOCK), bias, NEG)


def _swa_bias(rel_table):
    qi = np.arange(SWA_BLOCK)[:, None]
    kj = np.arange(2 * SWA_BLOCK)[None, :]
    dist = qi + SWA_BLOCK - kj
    in_window = (dist >= 0) & (dist < SWA_BLOCK)
    bucket = np.where(in_window, _t5_bucket_table()[np.clip(dist, 0, REL_MAX_DIST - 1)], -1).astype(np.int32)
    return pl.pallas_call(
        _swa_bias_body,
        in_specs=[
            pl.BlockSpec(memory_space=pltpu.VMEM),
            pl.BlockSpec(memory_space=pltpu.SMEM),
        ],
        out_specs=pl.BlockSpec(memory_space=pltpu.VMEM),
        out_shape=jax.ShapeDtypeStruct((2, SWA_Q_HEADS, SWA_BLOCK, 2 * SWA_BLOCK), F32),
        name="swa_bias",
    )(jnp.asarray(bucket), rel_table)


def _swa_body(q_ref, kc_ref, kp_ref, vc_ref, vp_ref, bias_ref, sink_ref, o_ref):
    blk = SWA_BLOCK
    lane = lax.broadcasted_iota(jnp.int32, (2 * blk, 128), 1)
    lane_q = lax.broadcasted_iota(jnp.int32, (blk, 128), 1)
    kband = jnp.concatenate([kp_ref[...], kc_ref[...]], axis=0)
    vband = jnp.concatenate([vp_ref[...], vc_ref[...]], axis=0)

    def dup_half(x2, half):
        rolled = pltpu.roll(x2, 64, axis=1)
        first = lane < 64
        return jnp.where(first, x2, rolled) if half == 0 else jnp.where(first, rolled, x2)

    for kvh in range(SWA_KV_HEADS):
        pair_cols = slice((kvh // 2) * 128, (kvh // 2 + 1) * 128)
        k2 = dup_half(kband[:, pair_cols].astype(F32), kvh % 2).astype(BF16)
        v2 = dup_half(vband[:, pair_cols].astype(F32), kvh % 2).astype(BF16)
        parts = []
        for j in range(SWA_GROUP // 2):
            c0 = kvh * SWA_GROUP * SWA_DH + j * 128
            q2 = q_ref[:, c0:c0 + 128]
            zero = jnp.zeros_like(q2)
            parts.append(jnp.where(lane_q < 64, q2, zero))
            parts.append(jnp.where(lane_q < 64, zero, q2))
        qstack = jnp.concatenate(parts, axis=0)
        s_all = _dot_nt(qstack, k2)
        ps = []
        dens = []
        for g in range(SWA_GROUP):
            hq = kvh * SWA_GROUP + g
            sg = s_all[g * blk:(g + 1) * blk] + bias_ref[0, hq]
            sink = sink_ref[hq] * LOG2E
            m = jnp.maximum(jnp.max(sg, axis=-1, keepdims=True), sink)
            p = jnp.exp2(sg - m)
            dens.append(jnp.sum(p, axis=-1, keepdims=True) + jnp.exp2(sink - m))
            ps.append(p.astype(BF16))
        o_all = _dot(jnp.concatenate(ps, axis=0), v2)
        for j in range(SWA_GROUP // 2):
            oa = o_all[(2 * j) * blk:(2 * j + 1) * blk] / dens[2 * j]
            ob = o_all[(2 * j + 1) * blk:(2 * j + 2) * blk] / dens[2 * j + 1]
            c0 = kvh * SWA_GROUP * SWA_DH + j * 128
            o_ref[:, c0:c0 + 128] = jnp.where(lane_q < 64, oa, ob).astype(o_ref.dtype)


def _swa_attention(p, bias, sinks, b, s):
    nb = s // SWA_BLOCK
    kvw = SWA_KV_HEADS * SWA_DH
    row = lambda bi, n: bi * nb + n
    prow = lambda bi, n: bi * nb + jnp.maximum(n - 1, 0)
    return pl.pallas_call(
        _swa_body,
        grid=(b, nb),
        in_specs=[
            pl.BlockSpec((SWA_BLOCK, D_MODEL), lambda bi, n: (row(bi, n), COL_QS // D_MODEL)),
            pl.BlockSpec((SWA_BLOCK, kvw), lambda bi, n: (row(bi, n), COL_KS // kvw)),
            pl.BlockSpec((SWA_BLOCK, kvw), lambda bi, n: (prow(bi, n), COL_KS // kvw)),
            pl.BlockSpec((SWA_BLOCK, kvw), lambda bi, n: (row(bi, n), COL_VS // kvw)),
            pl.BlockSpec((SWA_BLOCK, kvw), lambda bi, n: (prow(bi, n), COL_VS // kvw)),
            pl.BlockSpec((1, SWA_Q_HEADS, SWA_BLOCK, 2 * SWA_BLOCK),
                         lambda bi, n: (jnp.minimum(n, 1), 0, 0, 0)),
            pl.BlockSpec(memory_space=pltpu.SMEM),
        ],
        out_specs=pl.BlockSpec((SWA_BLOCK, D_MODEL), lambda bi, n: (row(bi, n), 0)),
        out_shape=jax.ShapeDtypeStruct((b * s, D_MODEL), BF16),
        compiler_params=_cparams(("parallel", "arbitrary")),
        name="swa_attn",
    )(p, p, p, p, p, bias, sinks)


def _merge_body(oa_ref, ob_ref, wa_ref, wb_ref, g0_ref, g1_ref, o_ref):
    ya = _dot(oa_ref[...], wa_ref[...])
    yb = _dot(ob_ref[...], wb_ref[...])
    g0 = jax.nn.sigmoid(g0_ref[...].astype(F32))
    g1 = jax.nn.sigmoid(g1_ref[...].astype(F32))
    o_ref[...] = (g0 * ya + g1 * yb).astype(o_ref.dtype)


def _merge(oa, ob, wa, wb, p, tm, tn):
    m = oa.shape[0]
    nj = D_MODEL // tn
    return pl.pallas_call(
        _merge_body,
        grid=(m // tm, nj),
        in_specs=[
            pl.BlockSpec((tm, D_MODEL), lambda i, j: (i, 0)),
            pl.BlockSpec((tm, D_MODEL), lambda i, j: (i, 0)),
            pl.BlockSpec((D_MODEL, tn), lambda i, j: (0, j)),
            pl.BlockSpec((D_MODEL, tn), lambda i, j: (0, j)),
            pl.BlockSpec((tm, tn), lambda i, j: (i, COL_GATES // tn + j)),
            pl.BlockSpec((tm, tn), lambda i, j: (i, COL_GATES // tn + nj + j)),
        ],
        out_specs=pl.BlockSpec((tm, tn), lambda i, j: (i, j)),
        out_shape=jax.ShapeDtypeStruct((m, D_MODEL), BF16),
        compiler_params=_cparams(("parallel", "arbitrary")),
        name="gated_merge",
    )(oa, ob, wa, wb, p, p)


def _out_proj_body(mg_ref, w_ref, x_ref, g_ref, o_ref):
    y = _dot(mg_ref[...], w_ref[...])
    o_ref[...] = x_ref[...] + _rms(y, g_ref[...])


def _out_proj(mg, w, x2, gain, tm):
    m = mg.shape[0]
    return pl.pallas_call(
        _out_proj_body,
        grid=(m // tm,),
        in_specs=[
            pl.BlockSpec((tm, D_MODEL), lambda i: (i, 0)),
            pl.BlockSpec((D_MODEL, D_MODEL), lambda i: (0, 0)),
            pl.BlockSpec((tm, D_MODEL), lambda i: (i, 0)),
            pl.BlockSpec((1, D_MODEL), lambda i: (0, 0)),
        ],
        out_specs=pl.BlockSpec((tm, D_MODEL), lambda i: (i, 0)),
        out_shape=jax.ShapeDtypeStruct((m, D_MODEL), F32),
        compiler_params=_cparams(("parallel",)),
        name="out_proj",
    )(mg, w, x2, gain)


HALO = 16


def _ffn_body(x_ref, xh_ref, gpre_ref, wa_ref, wb_ref, cw_ref, cb_ref, wd_ref, gpost_ref, o_ref,
              hn_ref, a_ref, acc_ref, *, tiles_per_seq):
    i = pl.program_id(0)
    j = pl.program_id(1)
    tm = x_ref.shape[0]

    @pl.when(j == 0)
    def _():
        halo = _rms(xh_ref[...], gpre_ref[...])
        halo = jnp.where(i % tiles_per_seq == 0, jnp.zeros_like(halo), halo)
        hn_ref[0:HALO, :] = halo.astype(BF16)
        hn_ref[HALO:, :] = _rms(x_ref[...], gpre_ref[...]).astype(BF16)
        acc_ref[...] = jnp.zeros_like(acc_ref)

    a_ref[...] = _dot(hn_ref[...], wa_ref[...])
    gate = _dot(hn_ref[HALO:, :], wb_ref[...])
    c = cb_ref[...] + cw_ref[0:1, :] * a_ref[pl.ds(HALO - 2, tm), :]
    c = c + cw_ref[1:2, :] * a_ref[pl.ds(HALO - 1, tm), :]
    c = c + cw_ref[2:3, :] * a_ref[pl.ds(HALO, tm), :]
    act = (jax.nn.gelu(c, approximate=True) * gate).astype(BF16)
    acc_ref[...] += _dot(act, wd_ref[...])

    @pl.when(j == pl.num_programs(1) - 1)
    def _():
        o_ref[...] = x_ref[...] + _rms(acc_ref[...], gpost_ref[...])


def _ffn(x1, gain_pre, w_up, conv_w, conv_b, w_down, gain_post, s, tm, tn):
    m = x1.shape[0]
    nj = D_FF // tn
    hb = tm // HALO
    return pl.pallas_call(
        functools.partial(_ffn_body, tiles_per_seq=s // tm),
        grid=(m // tm, nj),
        in_specs=[
            pl.BlockSpec((tm, D_MODEL), lambda i, j: (i, 0)),
            pl.BlockSpec((HALO, D_MODEL), lambda i, j: (jnp.maximum(i * hb - 1, 0), 0)),
            pl.BlockSpec((1, D_MODEL), lambda i, j: (0, 0)),
            pl.BlockSpec((D_MODEL, tn), lambda i, j: (0, j)),
            pl.BlockSpec((D_MODEL, tn), lambda i, j: (0, nj + j)),
            pl.BlockSpec((3, tn), lambda i, j: (0, j)),
            pl.BlockSpec((1, tn), lambda i, j: (0, j)),
            pl.BlockSpec((tn, D_MODEL), lambda i, j: (j, 0)),
            pl.BlockSpec((1, D_MODEL), lambda i, j: (0, 0)),
        ],
        out_specs=pl.BlockSpec((tm, D_MODEL), lambda i, j: (i, 0)),
        out_shape=jax.ShapeDtypeStruct((m, D_MODEL), F32),
        scratch_shapes=[
            pltpu.VMEM((HALO + tm, D_MODEL), BF16),
            pltpu.VMEM((HALO + tm, tn), F32),
            pltpu.VMEM((tm, D_MODEL), F32),
        ],
        compiler_params=_cparams(("parallel", "arbitrary")),
        name="ffn_fused",
    )(x1, x1, gain_pre, w_up, w_up, conv_w, conv_b, w_down, gain_post)


def _prep_w_in_body(w_ref, o_ref):
    def put(dst, src, width, scale=None):
        piece = w_ref[:, src:src + width]
        if scale is not None:
            piece = piece * scale
        o_ref[:, dst:dst + width] = piece.astype(BF16)

    put(COL_GATES, 3392, 4096)
    put(COL_QS, 832, 2048, SWA_QSCALE)
    put(COL_CQ, 0, 512)
    put(COL_KS, 2880, 256)
    put(COL_VS, 3136, 256)
    put(COL_CKV, 512, 256)
    zeros = jnp.zeros((w_ref.shape[0], 64), BF16)
    put(COL_KRA, 768, 64)
    o_ref[:, COL_KRA + 64:COL_KRA + 128] = zeros
    put(COL_KRB, 800, 32)
    put(COL_KRB + 32, 768, 32)
    o_ref[:, COL_KRB + 64:COL_KRB + 128] = zeros


def _prep_w_in(w_in):
    rows, cols = w_in.shape
    tr = 256
    return pl.pallas_call(
        _prep_w_in_body,
        grid=(rows // tr,),
        in_specs=[pl.BlockSpec((tr, cols), lambda i: (i, 0))],
        out_specs=pl.BlockSpec((tr, IN_COLS_PAD), lambda i: (i, 0)),
        out_shape=jax.ShapeDtypeStruct((rows, IN_COLS_PAD), BF16),
        compiler_params=_cparams(("parallel",)),
        name="w_in_prep",
    )(w_in)


def _prep_w_q_up(w):
    w = w.reshape(MLA_Q_RANK, MLA_HEADS, MLA_NOPE + MLA_ROPE)
    nope = w[:, :, :MLA_NOPE]
    rope = w[:, :, MLA_NOPE:]
    rope_sw = jnp.concatenate([rope[:, :, 32:], rope[:, :, :32]], axis=2)
    wq = jnp.concatenate([nope, rope, rope_sw], axis=2)
    return jnp.transpose(wq, (1, 2, 0)).astype(BF16)


def _prep_w_kv_up(w):
    w = w.reshape(MLA_KV_RANK, MLA_HEADS, MLA_NOPE + MLA_V)
    wk = w[:, :, :MLA_NOPE].reshape(MLA_KV_RANK, MLA_HEADS * MLA_NOPE).astype(BF16)
    wvt = jnp.transpose(w[:, :, MLA_NOPE:], (1, 2, 0)).astype(BF16)
    return wk, wvt


def _rope_tables(s):
    inv = ROPE_THETA ** (-jnp.arange(0, MLA_ROPE, 2, dtype=F32) / MLA_ROPE)
    ang = jnp.arange(s).astype(F32)[:, None] * inv[None, :]
    cos, sin = jnp.cos(ang), jnp.sin(ang)
    z = jnp.zeros((s, 64), F32)
    ck = jnp.concatenate([cos, cos, z], axis=1)
    sk = jnp.concatenate([-sin, sin, z], axis=1)
    cq2 = jnp.concatenate([cos, cos], axis=1).T
    sq2 = jnp.concatenate([-sin, sin], axis=1).T
    return cq2, sq2, ck, sk


def _tile(n, pref):
    t = min(n, pref)
    assert n % t == 0, (n, pref)
    return t


TILE_IN_PROJ = (1024, 1280)
TILE_QKV_ROWS = 512
TILE_MLA = 2048
TILE_MERGE = (1024, 1024)
TILE_OUT_ROWS = 512
TILE_FFN = (512, 512)


def kernel(x, norm_mix_pre, norm_mix_post, norm_ffn_pre, norm_ffn_post, w_in, mla_q_norm, mla_w_q_up, mla_kv_norm, mla_w_kv_up, swa_sinks, rel_bias_table, w_o_mla, w_o_swa, w_out, ffn_w_up, ffn_conv_w, ffn_conv_b, ffn_w_down):
    b, s, d = x.shape
    assert d == D_MODEL and s % SWA_BLOCK == 0
    depth = w_in.shape[0]
    m = b * s
    x2 = x.reshape(m, d)
    cq2, sq2, ck, sk = _rope_tables(s)
    bias = _swa_bias(rel_bias_table.astype(F32))
    for l in range(depth):
        p = _in_proj(x2, norm_mix_pre[l][None], _prep_w_in(w_in[l]), _tile(m, TILE_IN_PROJ[0]), TILE_IN_PROJ[1])
        wk, wvt = _prep_w_kv_up(mla_w_kv_up[l])
        qt, k, vt, qn2, kn2 = _qkv_up(p, mla_q_norm[l][None], mla_kv_norm[l][None], _prep_w_q_up(mla_w_q_up[l]),
                                      wk, wvt, cq2, sq2, ck, sk, b, s, _tile(s, TILE_QKV_ROWS))
        o_a = _mla_attention(qt, k, vt, qn2, kn2, _tile(s, TILE_MLA)).reshape(m, d)
        o_b = _swa_attention(p, bias, swa_sinks[l].astype(F32), b, s)
        mg = _merge(o_a, o_b, w_o_mla[l].astype(BF16), w_o_swa[l].astype(BF16), p,
                    _tile(m, TILE_MERGE[0]), TILE_MERGE[1])
        x1 = _out_proj(mg, w_out[l].astype(BF16), x2, norm_mix_post[l][None], _tile(m, TILE_OUT_ROWS))
        x2 = _ffn(x1, norm_ffn_pre[l][None], ffn_w_up[l].astype(BF16), ffn_conv_w[l], ffn_conv_b[l][None],
                  ffn_w_down[l].astype(BF16), norm_ffn_post[l][None], s, _tile(s, TILE_FFN[0]), TILE_FFN[1])
    return x2.reshape(b, s, d)
```

```python
import functools
import math

import numpy as np
import jax
import jax.numpy as jnp
from jax import lax
from jax.experimental import pallas as pl
from jax.experimental.pallas import tpu as pltpu

F32 = jnp.float32
BF16 = jnp.bfloat16

D_MODEL = 2048
MLA_HEADS = 16
MLA_Q_RANK = 512
MLA_KV_RANK = 256
MLA_NOPE = 128
MLA_ROPE = 64
MLA_V = 128
MLA_QK_PAD = 256
ROPE_THETA = 10000.0
SWA_Q_HEADS = 32
SWA_KV_HEADS = 4
SWA_GROUP = 8
SWA_DH = 64
SWA_BLOCK = 128
REL_BUCKETS = 32
REL_MAX_DIST = 128
D_FF = 5632
EPS = 1e-6
NEG = -1e30
LOG2E = 1.4426950408889634
SWA_QSCALE = SWA_DH ** -0.5 * LOG2E

COL_GATES = 0
COL_QS = 4096
COL_CQ = 6144
COL_KS = 6656
COL_VS = 6912
COL_CKV = 7168
COL_KRA = 7424
COL_KRB = 7552
IN_COLS_PAD = 7680

VMEM_LIMIT = 56 * 1024 * 1024


def _cparams(sem, flags=None):
    return pltpu.CompilerParams(dimension_semantics=sem, vmem_limit_bytes=VMEM_LIMIT, flags=flags)


def _rms(x, g):
    return x * lax.rsqrt(jnp.mean(x * x, axis=-1, keepdims=True) + EPS) * g


def _dot(a, b):
    return jnp.dot(a, b, preferred_element_type=F32)


def _dot_nt(a, b):
    return lax.dot_general(a, b, (((1,), (1,)), ((), ())), preferred_element_type=F32)


def _in_proj_body(x_ref, g_ref, w_ref, o_ref, hn_ref):
    @pl.when(pl.program_id(1) == 0)
    def _():
        hn_ref[...] = _rms(x_ref[...], g_ref[...]).astype(BF16)

    o_ref[...] = _dot(hn_ref[...], w_ref[...]).astype(o_ref.dtype)


def _in_proj(x2, gain, w, tm, tn):
    m = x2.shape[0]
    n = w.shape[1]
    return pl.pallas_call(
        _in_proj_body,
        grid=(m // tm, n // tn),
        in_specs=[
            pl.BlockSpec((tm, D_MODEL), lambda i, j: (i, 0)),
            pl.BlockSpec((1, D_MODEL), lambda i, j: (0, 0)),
            pl.BlockSpec((D_MODEL, tn), lambda i, j: (0, j)),
        ],
        out_specs=pl.BlockSpec((tm, tn), lambda i, j: (i, j)),
        out_shape=jax.ShapeDtypeStruct((m, n), BF16),
        scratch_shapes=[pltpu.VMEM((tm, D_MODEL), BF16)],
        compiler_params=_cparams(("parallel", "arbitrary")),
        name="in_proj",
    )(x2, gain, w)


def _qkv_up_body(cq_ref, ckv_ref, kra_ref, krb_ref, gq_ref, gkv_ref, wqt_ref, wk_ref, wvt_ref,
                 cq2_ref, sq2_ref, ck_ref, sk_ref, qt_ref, k_ref, vt_ref, qn2_ref, kn2_ref, *, qscale):
    ts = cq_ref.shape[0]
    cqn = _rms(cq_ref[...].astype(F32), gq_ref[...]).astype(BF16)
    ckvn = _rms(ckv_ref[...].astype(F32), gkv_ref[...]).astype(BF16)
    cq2 = cq2_ref[...]
    sq2 = sq2_ref[...]
    zeros_q = jnp.zeros((MLA_QK_PAD - MLA_NOPE - MLA_ROPE, ts), BF16)
    ones_v = jnp.ones((MLA_ACC_ROWS - MLA_V, ts), BF16)
    ones_n = jnp.ones((8, 128), BF16)
    for h in range(MLA_HEADS):
        qf = _dot_nt(wqt_ref[h], cqn) * qscale
        rope = qf[MLA_NOPE:MLA_NOPE + MLA_ROPE] * cq2 + qf[MLA_NOPE + MLA_ROPE:] * sq2
        qt_ref[0, h, 0:MLA_NOPE, :] = qf[0:MLA_NOPE].astype(BF16)
        qt_ref[0, h, MLA_NOPE:MLA_NOPE + MLA_ROPE, :] = rope.astype(BF16)
        qt_ref[0, h, MLA_NOPE + MLA_ROPE:, :] = zeros_q
        qn2 = (jnp.sum(qf[0:MLA_NOPE] * qf[0:MLA_NOPE], axis=0, keepdims=True)
               + jnp.sum(rope * rope, axis=0, keepdims=True))
        qn2_ref[0, h] = jnp.broadcast_to(qn2, (8, ts))
    kr = kra_ref[...].astype(F32) * ck_ref[...] + krb_ref[...].astype(F32) * sk_ref[...]
    lane = lax.broadcasted_iota(jnp.int32, kr.shape, 1)
    kr_aug = jnp.where(lane == MLA_ROPE, 1.0, kr).astype(BF16)
    kall = _dot(ckvn, wk_ref[...])
    for h in range(MLA_HEADS):
        kh = kall[:, h * MLA_NOPE:(h + 1) * MLA_NOPE]
        k_ref[0, h, :, 0:MLA_NOPE] = kh.astype(BF16)
        k_ref[0, h, :, MLA_NOPE:] = kr_aug
        kn2_ref[0, h] = _dot_nt(ones_n, (kh * kh + kr * kr).astype(BF16))
        vt_ref[0, h, 0:MLA_V, :] = _dot_nt(wvt_ref[h], ckvn).astype(BF16)
        vt_ref[0, h, MLA_V:, :] = ones_v


def _qkv_up(p, gq, gkv, wqt, wk, wvt, cq2, sq2, ck, sk, b, s, ts):
    ns = s // ts
    row = lambda bi, i: bi * ns + i
    qscale = float((MLA_NOPE + MLA_ROPE) ** -0.5 * LOG2E)
    full = lambda shape: pl.BlockSpec(shape, lambda bi, i: (0,) * len(shape))
    return pl.pallas_call(
        functools.partial(_qkv_up_body, qscale=qscale),
        grid=(b, ns),
        in_specs=[
            pl.BlockSpec((ts, MLA_Q_RANK), lambda bi, i: (row(bi, i), COL_CQ // MLA_Q_RANK)),
            pl.BlockSpec((ts, MLA_KV_RANK), lambda bi, i: (row(bi, i), COL_CKV // MLA_KV_RANK)),
            pl.BlockSpec((ts, 128), lambda bi, i: (row(bi, i), COL_KRA // 128)),
            pl.BlockSpec((ts, 128), lambda bi, i: (row(bi, i), COL_KRB // 128)),
            full((1, MLA_Q_RANK)),
            full((1, MLA_KV_RANK)),
            full((MLA_HEADS, MLA_QK_PAD, MLA_Q_RANK)),
            full((MLA_KV_RANK, MLA_HEADS * MLA_NOPE)),
            full((MLA_HEADS, MLA_V, MLA_KV_RANK)),
            pl.BlockSpec((MLA_ROPE, ts), lambda bi, i: (0, i)),
            pl.BlockSpec((MLA_ROPE, ts), lambda bi, i: (0, i)),
            pl.BlockSpec((ts, 128), lambda bi, i: (i, 0)),
            pl.BlockSpec((ts, 128), lambda bi, i: (i, 0)),
        ],
        out_specs=[
            pl.BlockSpec((1, MLA_HEADS, MLA_QK_PAD, ts), lambda bi, i: (bi, 0, 0, i)),
            pl.BlockSpec((1, MLA_HEADS, ts, MLA_QK_PAD), lambda bi, i: (bi, 0, i, 0)),
            pl.BlockSpec((1, MLA_HEADS, MLA_ACC_ROWS, ts), lambda bi, i: (bi, 0, 0, i)),
            pl.BlockSpec((1, MLA_HEADS, 8, ts), lambda bi, i: (bi, 0, 0, i)),
            pl.BlockSpec((1, MLA_HEADS, 8, ts), lambda bi, i: (bi, 0, 0, i)),
        ],
        out_shape=[
            jax.ShapeDtypeStruct((b, MLA_HEADS, MLA_QK_PAD, s), BF16),
            jax.ShapeDtypeStruct((b, MLA_HEADS, s, MLA_QK_PAD), BF16),
            jax.ShapeDtypeStruct((b, MLA_HEADS, MLA_ACC_ROWS, s), BF16),
            jax.ShapeDtypeStruct((b, MLA_HEADS, 8, s), F32),
            jax.ShapeDtypeStruct((b, MLA_HEADS, 8, s), F32),
        ],
        compiler_params=_cparams(("parallel", "parallel")),
        name="qkv_up",
    )(p, p, p, p, gq, gkv, wqt, wk, wvt, cq2, sq2, ck, sk)


MLA_GW = 256
MLA_QK_ROWS = 512
MLA_ACC_ROWS = 144
MLA_SHIFT_ROW = MLA_NOPE + MLA_ROPE
MLA_UB_MARGIN = 1.03125
MLA_L_MIN = 2.0 ** -100


def _mla_body(qt_ref, k_ref, vt_ref, qn2_ref, kn2_ref, o_ref, acc_ref, qs_ref, m_ref, p0, p1, kmax_ref, *, t):
    i = pl.program_id(2)
    n_groups = t // MLA_GW
    p_s = (p0, p1)

    def gcols(g):
        return slice(g * MLA_GW, (g + 1) * MLA_GW)

    @pl.when(i == 0)
    def _():
        kmax2 = jnp.max(jnp.max(kn2_ref[0, 0], axis=1, keepdims=True), axis=0, keepdims=True)
        kmax_ref[...] = jnp.broadcast_to(kmax2, kmax_ref.shape)

    ub = jnp.sqrt(qn2_ref[0, 0, 0:1, :] * kmax_ref[:, 0:1]) * MLA_UB_MARGIN
    qs_ref[...] = qt_ref[0, 0]
    row = lax.broadcasted_iota(jnp.int32, (16, t), 0)
    qs_ref[MLA_SHIFT_ROW:MLA_SHIFT_ROW + 16, :] = jnp.where(row == 0, -ub, 0.0).astype(BF16)
    acc_ref[...] = jnp.zeros_like(acc_ref)

    def probs(k0, rows, g, masked):
        ps = []
        for r in range(0, rows, MLA_QK_ROWS):
            n = min(MLA_QK_ROWS, rows - r)
            st = _dot(k_ref[0, 0, pl.ds(k0 + r, n), :], qs_ref[:, gcols(g)])
            if masked and r + n > rows - MLA_GW:
                kpos = r + lax.broadcasted_iota(jnp.int32, (n, MLA_GW), 0)
                qpos = rows - MLA_GW + lax.broadcasted_iota(jnp.int32, (n, MLA_GW), 1)
                st = jnp.where(kpos <= qpos, st, NEG)
            ps.append(jnp.exp2(st).astype(BF16))
        return ps[0] if len(ps) == 1 else jnp.concatenate(ps, axis=0)

    def key_off(j):
        return pl.multiple_of(jnp.where(j == 0, i, j - 1) * t, t)

    def pv(j, slot, g):
        acc_ref[g] += _dot(vt_ref[0, 0, :, pl.ds(key_off(j), t)], p_s[slot][g])

    def step(k, slot):
        for g in range(n_groups):
            p_s[slot][g] = probs(key_off(k), t, g, False)
            pv(k - 1, 1 - slot, g)

    for g in range(n_groups):
        rows = (g + 1) * MLA_GW
        p0[g, 0:rows, :] = probs(pl.multiple_of(i * t, t), rows, g, True)
        if rows < t:
            p0[g, rows:, :] = jnp.zeros((t - rows, MLA_GW), BF16)

    body_steps = 4 if t <= 1024 else 2

    def body(u, carry):
        for d in range(body_steps):
            step(body_steps * u + d + 1, (d + 1) % 2)
        return carry

    lax.fori_loop(0, i // body_steps, body, 0)

    if body_steps == 4:
        @pl.when((i // 2) % 2 == 1)
        def _():
            k_base = (i // 4) * 4
            step(k_base + 1, 1)
            step(k_base + 2, 0)

    @pl.when(i % 2 == 1)
    def _():
        step(i, 1)
        for g in range(n_groups):
            pv(i, 1, g)

    @pl.when(i % 2 == 0)
    def _():
        for g in range(n_groups):
            pv(i, 0, g)

    l_min = acc_ref[0, MLA_V:MLA_V + 1, :]
    for g in range(1, n_groups):
        l_min = jnp.minimum(l_min, acc_ref[g, MLA_V:MLA_V + 1, :])
    l_min = jnp.min(l_min, axis=1, keepdims=True)

    @pl.when(jnp.logical_not(l_min[0, 0] >= MLA_L_MIN))
    def _():
        acc_ref[...] = jnp.zeros_like(acc_ref)
        m_ref[...] = jnp.full_like(m_ref, NEG)

        def exact_chunk(c, carry):
            k0 = pl.multiple_of(c * MLA_QK_ROWS, MLA_QK_ROWS)
            kc = k_ref[0, 0, pl.ds(k0, MLA_QK_ROWS), :]
            vt1 = vt_ref[0, 0, :, pl.ds(k0, MLA_QK_ROWS)]
            for g in range(n_groups):
                st = _dot(kc, qt_ref[0, 0, :, gcols(g)])
                kpos = k0 + lax.broadcasted_iota(jnp.int32, st.shape, 0)
                qpos = i * t + g * MLA_GW + lax.broadcasted_iota(jnp.int32, st.shape, 1)
                st = jnp.where(kpos <= qpos, st, NEG)
                m_old = m_ref[g]
                m_new = jnp.maximum(m_old, jnp.max(st, axis=0, keepdims=True))
                m_ref[g] = m_new
                p = jnp.exp2(st - m_new).astype(BF16)
                acc_ref[g] = jnp.exp2(m_old - m_new) * acc_ref[g] + _dot(vt1, p)
            return carry

        lax.fori_loop(0, (i + 1) * (t // MLA_QK_ROWS), exact_chunk, 0)

    for g in range(n_groups):
        out_t = acc_ref[g, 0:MLA_V, :] / acc_ref[g, MLA_V:MLA_V + 1, :]
        o_ref[0, g * MLA_GW:(g + 1) * MLA_GW, :] = out_t.T.astype(o_ref.dtype)


def _mla_attention(qt, k, vt, qn2, kn2, t):
    b, h, _, s = qt.shape
    n_groups = t // MLA_GW
    return pl.pallas_call(
        functools.partial(_mla_body, t=t),
        grid=(b, h, s // t),
        in_specs=[
            pl.BlockSpec((1, 1, MLA_QK_PAD, t), lambda bi, hi, i: (bi, hi, 0, i)),
            pl.BlockSpec((1, 1, s, MLA_QK_PAD), lambda bi, hi, i: (bi, hi, 0, 0)),
            pl.BlockSpec((1, 1, MLA_ACC_ROWS, s), lambda bi, hi, i: (bi, hi, 0, 0)),
            pl.BlockSpec((1, 1, 8, t), lambda bi, hi, i: (bi, hi, 0, i)),
            pl.BlockSpec((1, 1, 8, s), lambda bi, hi, i: (bi, hi, 0, 0)),
        ],
        out_specs=pl.BlockSpec((1, t, MLA_V), lambda bi, hi, i: (bi, i, hi)),
        out_shape=jax.ShapeDtypeStruct((b, s, h * MLA_V), BF16),
        scratch_shapes=[
            pltpu.VMEM((n_groups, MLA_ACC_ROWS, MLA_GW), F32),
            pltpu.VMEM((MLA_QK_PAD, t), BF16),
            pltpu.VMEM((n_groups, 1, MLA_GW), F32),
            pltpu.VMEM((n_groups, t, MLA_GW), BF16),
            pltpu.VMEM((n_groups, t, MLA_GW), BF16),
            pltpu.VMEM((1, 128), F32),
        ],
        compiler_params=_cparams(("parallel", "parallel", "arbitrary")),
        name="mla_attn",
    )(qt, k, vt, qn2, kn2)


def _t5_bucket_table():
    n = np.arange(REL_MAX_DIST)
    max_exact = REL_BUCKETS // 2
    large = max_exact + (np.log(np.maximum(n, 1).astype(np.float32) / max_exact)
                         / math.log(REL_MAX_DIST / max_exact) * (REL_BUCKETS - max_exact)).astype(np.int32)
    large = np.minimum(large, REL_BUCKETS - 1)
    return np.where(n < max_exact, n, large)


def _swa_bias_body(bucket_ref, table_ref, o_ref):
    bucket = bucket_ref[...]
    kj = lax.broadcasted_iota(jnp.int32, bucket.shape, 1)
    for h in range(SWA_Q_HEADS):
        bias = jnp.zeros(bucket.shape, F32)
        for bk in range(REL_BUCKETS):
            bias = jnp.where(bucket == bk, table_ref[bk, h], bias)
        valid = bucket >= 0
        bias = bias * LOG2E
        o_ref[1, h] = jnp.where(valid, bias, NEG)
        o_ref[0, h] = jnp.where(valid & (kj >= SWA_BLOCK), bias, NEG)


def _swa_bias(rel_table):
    qi = np.arange(SWA_BLOCK)[:, None]
    kj = np.arange(2 * SWA_BLOCK)[None, :]
    dist = qi + SWA_BLOCK - kj
    in_window = (dist >= 0) & (dist < SWA_BLOCK)
    bucket = np.where(in_window, _t5_bucket_table()[np.clip(dist, 0, REL_MAX_DIST - 1)], -1).astype(np.int32)
    return pl.pallas_call(
        _swa_bias_body,
        in_specs=[
            pl.BlockSpec(memory_space=pltpu.VMEM),
            pl.BlockSpec(memory_space=pltpu.SMEM),
        ],
        out_specs=pl.BlockSpec(memory_space=pltpu.VMEM),
        out_shape=jax.ShapeDtypeStruct((2, SWA_Q_HEADS, SWA_BLOCK, 2 * SWA_BLOCK), F32),
        name="swa_bias",
    )(jnp.asarray(bucket), rel_table)


def _swa_body(q_ref, kc_ref, kp_ref, vc_ref, vp_ref, bias_ref, sink_ref, o_ref):
    blk = SWA_BLOCK
    lane = lax.broadcasted_iota(jnp.int32, (2 * blk, 128), 1)
    lane_q = lax.broadcasted_iota(jnp.int32, (blk, 128), 1)
    kband = jnp.concatenate([kp_ref[...], kc_ref[...]], axis=0)
    vband = jnp.concatenate([vp_ref[...], vc_ref[...]], axis=0)

    def dup_half(x2, half):
        rolled = pltpu.roll(x2, 64, axis=1)
        first = lane < 64
        return jnp.where(first, x2, rolled) if half == 0 else jnp.where(first, rolled, x2)

    for kvh in range(SWA_KV_HEADS):
        pair_cols = slice((kvh // 2) * 128, (kvh // 2 + 1) * 128)
        k2 = dup_half(kband[:, pair_cols].astype(F32), kvh % 2).astype(BF16)
        v2 = dup_half(vband[:, pair_cols].astype(F32), kvh % 2).astype(BF16)
        parts = []
        for j in range(SWA_GROUP // 2):
            c0 = kvh * SWA_GROUP * SWA_DH + j * 128
            q2 = q_ref[:, c0:c0 + 128]
            zero = jnp.zeros_like(q2)
            parts.append(jnp.where(lane_q < 64, q2, zero))
            parts.append(jnp.where(lane_q < 64, zero, q2))
        qstack = jnp.concatenate(parts, axis=0)
        s_all = _dot_nt(qstack, k2)
        ps = []
        dens = []
        for g in range(SWA_GROUP):
            hq = kvh * SWA_GROUP + g
            sg = s_all[g * blk:(g + 1) * blk] + bias_ref[0, hq]
            sink = sink_ref[hq] * LOG2E
            m = jnp.maximum(jnp.max(sg, axis=-1, keepdims=True), sink)
            p = jnp.exp2(sg - m)
            dens.append(jnp.sum(p, axis=-1, keepdims=True) + jnp.exp2(sink - m))
            ps.append(p.astype(BF16))
        o_all = _dot(jnp.concatenate(ps, axis=0), v2)
        for j in range(SWA_GROUP // 2):
            oa = o_all[(2 * j) * blk:(2 * j + 1) * blk] / dens[2 * j]
            ob = o_all[(2 * j + 1) * blk:(2 * j + 2) * blk] / dens[2 * j + 1]
            c0 = kvh * SWA_GROUP * SWA_DH + j * 128
            o_ref[:, c0:c0 + 128] = jnp.where(lane_q < 64, oa, ob).astype(o_ref.dtype)


def _swa_attention(p, bias, sinks, b, s):
    nb = s // SWA_BLOCK
    kvw = SWA_KV_HEADS * SWA_DH
    row = lambda bi, n: bi * nb + n
    prow = lambda bi, n: bi * nb + jnp.maximum(n - 1, 0)
    return pl.pallas_call(
        _swa_body,
        grid=(b, nb),
        in_specs=[
            pl.BlockSpec((SWA_BLOCK, D_MODEL), lambda bi, n: (row(bi, n), COL_QS // D_MODEL)),
            pl.BlockSpec((SWA_BLOCK, kvw), lambda bi, n: (row(bi, n), COL_KS // kvw)),
            pl.BlockSpec((SWA_BLOCK, kvw), lambda bi, n: (prow(bi, n), COL_KS // kvw)),
            pl.BlockSpec((SWA_BLOCK, kvw), lambda bi, n: (row(bi, n), COL_VS // kvw)),
            pl.BlockSpec((SWA_BLOCK, kvw), lambda bi, n: (prow(bi, n), COL_VS // kvw)),
            pl.BlockSpec((1, SWA_Q_HEADS, SWA_BLOCK, 2 * SWA_BLOCK),
                         lambda bi, n: (jnp.minimum(n, 1), 0, 0, 0)),
            pl.BlockSpec(memory_space=pltpu.SMEM),
        ],
        out_specs=pl.BlockSpec((SWA_BLOCK, D_MODEL), lambda bi, n: (row(bi, n), 0)),
        out_shape=jax.ShapeDtypeStruct((b * s, D_MODEL), BF16),
        compiler_params=_cparams(("parallel", "arbitrary")),
        name="swa_attn",
    )(p, p, p, p, p, bias, sinks)


def _merge_out_body(oa_ref, ob_ref, wa_ref, wb_ref, g0_ref, g1_ref, wo_ref, x_ref, gain_ref, o_ref):
    ya = _dot(oa_ref[...], wa_ref[...])
    yb = _dot(ob_ref[...], wb_ref[...])
    g0 = jax.nn.sigmoid(g0_ref[...].astype(F32))
    g1 = jax.nn.sigmoid(g1_ref[...].astype(F32))
    merged = (g0 * ya + g1 * yb).astype(BF16)
    y = _dot(merged, wo_ref[...])
    o_ref[...] = x_ref[...] + _rms(y, gain_ref[...])


def _merge_out(oa, ob, wa, wb, p, wo, x2, gain, tm):
    m = oa.shape[0]
    rows = lambda i: (i, 0)
    resident = lambda: pl.BlockSpec((D_MODEL, D_MODEL), lambda i: (0, 0), pipeline_mode=pl.Buffered(1))
    return pl.pallas_call(
        _merge_out_body,
        grid=(m // tm,),
        in_specs=[
            pl.BlockSpec((tm, D_MODEL), rows),
            pl.BlockSpec((tm, D_MODEL), rows),
            resident(),
            resident(),
            pl.BlockSpec((tm, D_MODEL), lambda i: (i, COL_GATES // D_MODEL)),
            pl.BlockSpec((tm, D_MODEL), lambda i: (i, COL_GATES // D_MODEL + 1)),
            resident(),
            pl.BlockSpec((tm, D_MODEL), rows),
            pl.BlockSpec((1, D_MODEL), lambda i: (0, 0)),
        ],
        out_specs=pl.BlockSpec((tm, D_MODEL), rows),
        out_shape=jax.ShapeDtypeStruct((m, D_MODEL), F32),
        compiler_params=_cparams(("parallel",)),
        name="merge_out_proj",
    )(oa, ob, wa, wb, p, p, wo, x2, gain)


HALO = 16


def _ffn_body(x_ref, xh_ref, gpre_ref, wa_ref, wb_ref, cw_ref, cb_ref, wd_ref, gpost_ref, o_ref,
              hn_ref, a_ref, acc_ref, *, tiles_per_seq):
    i = pl.program_id(0)
    j = pl.program_id(1)
    tm = x_ref.shape[0]

    @pl.when(j == 0)
    def _():
        halo = _rms(xh_ref[...], gpre_ref[...])
        halo = jnp.where(i % tiles_per_seq == 0, jnp.zeros_like(halo), halo)
        hn_ref[0:HALO, :] = halo.astype(BF16)
        hn_ref[HALO:, :] = _rms(x_ref[...], gpre_ref[...]).astype(BF16)
        acc_ref[...] = jnp.zeros_like(acc_ref)

    a_ref[...] = _dot(hn_ref[...], wa_ref[...])
    gate = _dot(hn_ref[HALO:, :], wb_ref[...])
    c = cb_ref[...] + cw_ref[0:1, :] * a_ref[pl.ds(HALO - 2, tm), :]
    c = c + cw_ref[1:2, :] * a_ref[pl.ds(HALO - 1, tm), :]
    c = c + cw_ref[2:3, :] * a_ref[pl.ds(HALO, tm), :]
    act = (jax.nn.gelu(c, approximate=True) * gate).astype(BF16)
    acc_ref[...] += _dot(act, wd_ref[...])

    @pl.when(j == pl.num_programs(1) - 1)
    def _():
        o_ref[...] = x_ref[...] + _rms(acc_ref[...], gpost_ref[...])


def _ffn(x1, gain_pre, w_up, conv_w, conv_b, w_down, gain_post, s, tm, tn):
    m = x1.shape[0]
    nj = D_FF // tn
    hb = tm // HALO
    return pl.pallas_call(
        functools.partial(_ffn_body, tiles_per_seq=s // tm),
        grid=(m // tm, nj),
        in_specs=[
            pl.BlockSpec((tm, D_MODEL), lambda i, j: (i, 0)),
            pl.BlockSpec((HALO, D_MODEL), lambda i, j: (jnp.maximum(i * hb - 1, 0), 0)),
            pl.BlockSpec((1, D_MODEL), lambda i, j: (0, 0)),
            pl.BlockSpec((D_MODEL, tn), lambda i, j: (0, j)),
            pl.BlockSpec((D_MODEL, tn), lambda i, j: (0, nj + j)),
            pl.BlockSpec((3, tn), lambda i, j: (0, j)),
            pl.BlockSpec((1, tn), lambda i, j: (0, j)),
            pl.BlockSpec((tn, D_MODEL), lambda i, j: (j, 0)),
            pl.BlockSpec((1, D_MODEL), lambda i, j: (0, 0)),
        ],
        out_specs=pl.BlockSpec((tm, D_MODEL), lambda i, j: (i, 0)),
        out_shape=jax.ShapeDtypeStruct((m, D_MODEL), F32),
        scratch_shapes=[
            pltpu.VMEM((HALO + tm, D_MODEL), BF16),
            pltpu.VMEM((HALO + tm, tn), F32),
            pltpu.VMEM((tm, D_MODEL), F32),
        ],
        compiler_params=_cparams(("parallel", "arbitrary")),
        name="ffn_fused",
    )(x1, x1, gain_pre, w_up, w_up, conv_w, conv_b, w_down, gain_post)


def _prep_w_in_body(w_ref, o_ref):
    def put(dst, src, width, scale=None):
        piece = w_ref[:, src:src + width]
        if scale is not None:
            piece = piece * scale
        o_ref[:, dst:dst + width] = piece.astype(BF16)

    put(COL_GATES, 3392, 4096)
    put(COL_QS, 832, 2048, SWA_QSCALE)
    put(COL_CQ, 0, 512)
    put(COL_KS, 2880, 256)
    put(COL_VS, 3136, 256)
    put(COL_CKV, 512, 256)
    zeros = jnp.zeros((w_ref.shape[0], 64), BF16)
    put(COL_KRA, 768, 64)
    o_ref[:, COL_KRA + 64:COL_KRA + 128] = zeros
    put(COL_KRB, 800, 32)
    put(COL_KRB + 32, 768, 32)
    o_ref[:, COL_KRB + 64:COL_KRB + 128] = zeros


def _prep_w_in(w_in):
    rows, cols = w_in.shape
    tr = 256
    return pl.pallas_call(
        _prep_w_in_body,
        grid=(rows // tr,),
        in_specs=[pl.BlockSpec((tr, cols), lambda i: (i, 0))],
        out_specs=pl.BlockSpec((tr, IN_COLS_PAD), lambda i: (i, 0)),
        out_shape=jax.ShapeDtypeStruct((rows, IN_COLS_PAD), BF16),
        compiler_params=_cparams(("parallel",)),
        name="w_in_prep",
    )(w_in)


def _prep_w_q_up(w):
    w = w.reshape(MLA_Q_RANK, MLA_HEADS, MLA_NOPE + MLA_ROPE)
    nope = w[:, :, :MLA_NOPE]
    rope = w[:, :, MLA_NOPE:]
    rope_sw = jnp.concatenate([rope[:, :, 32:], rope[:, :, :32]], axis=2)
    wq = jnp.concatenate([nope, rope, rope_sw], axis=2)
    return jnp.transpose(wq, (1, 2, 0)).astype(BF16)


def _prep_w_kv_up(w):
    w = w.reshape(MLA_KV_RANK, MLA_HEADS, MLA_NOPE + MLA_V)
    wk = w[:, :, :MLA_NOPE].reshape(MLA_KV_RANK, MLA_HEADS * MLA_NOPE).astype(BF16)
    wvt = jnp.transpose(w[:, :, MLA_NOPE:], (1, 2, 0)).astype(BF16)
    return wk, wvt


def _rope_tables(s):
    inv = ROPE_THETA ** (-jnp.arange(0, MLA_ROPE, 2, dtype=F32) / MLA_ROPE)
    ang = jnp.arange(s).astype(F32)[:, None] * inv[None, :]
    cos, sin = jnp.cos(ang), jnp.sin(ang)
    z = jnp.zeros((s, 64), F32)
    ck = jnp.concatenate([cos, cos, z], axis=1)
    sk = jnp.concatenate([-sin, sin, z], axis=1)
    cq2 = jnp.concatenate([cos, cos], axis=1).T
    sq2 = jnp.concatenate([-sin, sin], axis=1).T
    return cq2, sq2, ck, sk


def _tile(n, pref):
    t = min(n, pref)
    assert n % t == 0, (n, pref)
    return t


TILE_IN_PROJ = (1024, 1280)
TILE_QKV_ROWS = 512
TILE_MLA = 2048
TILE_MERGE_OUT_ROWS = 256
TILE_FFN = (512, 512)


def kernel(x, norm_mix_pre, norm_mix_post, norm_ffn_pre, norm_ffn_post, w_in, mla_q_norm, mla_w_q_up, mla_kv_norm, mla_w_kv_up, swa_sinks, rel_bias_table, w_o_mla, w_o_swa, w_out, ffn_w_up, ffn_conv_w, ffn_conv_b, ffn_w_down):
    b, s, d = x.shape
    assert d == D_MODEL and s % SWA_BLOCK == 0
    depth = w_in.shape[0]
    m = b * s
    x2 = x.reshape(m, d)
    cq2, sq2, ck, sk = _rope_tables(s)
    bias = _swa_bias(rel_bias_table.astype(F32))
    for l in range(depth):
        p = _in_proj(x2, norm_mix_pre[l][None], _prep_w_in(w_in[l]), _tile(m, TILE_IN_PROJ[0]), TILE_IN_PROJ[1])
        wk, wvt = _prep_w_kv_up(mla_w_kv_up[l])
        qt, k, vt, qn2, kn2 = _qkv_up(p, mla_q_norm[l][None], mla_kv_norm[l][None], _prep_w_q_up(mla_w_q_up[l]),
                                      wk, wvt, cq2, sq2, ck, sk, b, s, _tile(s, TILE_QKV_ROWS))
        o_a = _mla_attention(qt, k, vt, qn2, kn2, _tile(s, TILE_MLA)).reshape(m, d)
        o_b = _swa_attention(p, bias, swa_sinks[l].astype(F32), b, s)
        x1 = _merge_out(o_a, o_b, w_o_mla[l].astype(BF16), w_o_swa[l].astype(BF16), p, w_out[l].astype(BF16),
                        x2, norm_mix_post[l][None], _tile(m, TILE_MERGE_OUT_ROWS))
        x2 = _ffn(x1, norm_ffn_pre[l][None], ffn_w_up[l].astype(BF16), ffn_conv_w[l], ffn_conv_b[l][None],
                  ffn_w_down[l].astype(BF16), norm_ffn_post[l][None], s, _tile(s, TILE_FFN[0]), TILE_FFN[1])
    return x2.reshape(b, s, d)
```
